```python
import jax, jax.numpy as jnp
from jax import lax
import numpy as np

D_MODEL = 2048
BATCH = 4
SEQ = 4096
DEPTH = 4
DEC_BATCH = 32
DEC_SEQ = 64
PAST_LEN = 1024

CHUNK = 64
Q_BLOCK = 128
HEAD_DIM = 128
N_AB = (DEPTH + 1) // 2
N_CD = DEPTH // 2
H_MLA = 8
Q_RANK = 512
KV_RANK = 256
NOPE_DIM = 128
ROPE_DIM = 64
QK_HEAD = NOPE_DIM + ROPE_DIM
V_DIM = 128
ROPE_THETA = 10000.0
MLA_SCALE = QK_HEAD ** -0.5
H_SB = 8
SB_SCALE = HEAD_DIM ** -0.5
H_BAND = 8
BAND_LEFT = 8
REL_CLIP = 128
H_FOX = 8
F_BIAS_INIT = 3.0
ATT_SCALE = HEAD_DIM ** -0.5
D_FF = 5632
EPS = 1e-6
NEG_INF = -1e30

AB_SIZES = (Q_RANK, KV_RANK, ROPE_DIM, H_SB * HEAD_DIM, H_SB * HEAD_DIM, H_SB * HEAD_DIM)
AB_IN = Q_RANK + KV_RANK + ROPE_DIM + 3 * H_SB * HEAD_DIM
AB_OUT = H_MLA * V_DIM + H_SB * HEAD_DIM
CD_SIZES = (H_BAND * HEAD_DIM, H_BAND * HEAD_DIM, H_BAND * HEAD_DIM,
            H_FOX * HEAD_DIM, H_FOX * HEAD_DIM, H_FOX * HEAD_DIM, H_FOX)
CD_IN = 3 * H_BAND * HEAD_DIM + 3 * H_FOX * HEAD_DIM + H_FOX
CD_OUT = H_BAND * HEAD_DIM + H_FOX * HEAD_DIM

kernel_name = 'hybrid_streaming_encoder_step'


def split_cols(x, sizes):
    out, start = [], 0
    for s in sizes:
        out.append(x[..., start:start + s])
        start += s
    return out


def rms_norm(x, g):
    xf = x.astype(jnp.float32)
    y = xf * lax.rsqrt(jnp.mean(xf * xf, axis=-1, keepdims=True) + EPS)
    return (y * g.astype(jnp.float32)).astype(x.dtype)


def rope(x, pos):
    half = x.shape[-1] // 2
    inv = ROPE_THETA ** (-jnp.arange(half, dtype=jnp.float32) / half)
    ang = pos.astype(jnp.float32)[:, None] * inv[None, :]
    shape = (ang.shape[0],) + (1,) * (x.ndim - 3) + (half,)
    cos, sin = jnp.cos(ang).reshape(shape), jnp.sin(ang).reshape(shape)
    xf = x.astype(jnp.float32)
    x1, x2 = xf[..., :half], xf[..., half:]
    return jnp.concatenate([x1 * cos - x2 * sin, x2 * cos + x1 * sin], -1).astype(x.dtype)


def swiglu_ffn(x, g, w_gate, w_up, w_down):
    h = rms_norm(x, g)
    return (jax.nn.silu(h @ w_gate) * (h @ w_up)) @ w_down


def tail_rows(a, n):
    t = a.shape[1]
    if t >= n:
        return a[:, t - n:]
    pad = [(0, 0)] * a.ndim
    pad[1] = (n - t, 0)
    return jnp.pad(a, pad)


def softmax_attn(q, k, v, bias, mask, scale):
    s = jnp.einsum('...qhd,...khd->...hqk', q, k).astype(jnp.float32) * scale
    if bias is not None:
        s = s + bias
    s = jnp.where(mask, s, NEG_INF)
    p = jax.nn.softmax(s, axis=-1).astype(v.dtype)
    return jnp.einsum('...hqk,...khe->...qhe', p, v)


def stick_breaking_attn(q, k, v, q_pos, k_pos):
    z = jnp.einsum('bqhd,bkhd->bhqk', q, k).astype(jnp.float32) * SB_SCALE
    mask = k_pos[None, :] < q_pos[:, None]
    log_keep = jnp.where(mask, jax.nn.log_sigmoid(-z), 0.0)
    log_between = lax.cumsum(log_keep, axis=z.ndim - 1, reverse=True) - log_keep
    w = jnp.where(mask, jnp.exp(jax.nn.log_sigmoid(z) + log_between), 0.0).astype(v.dtype)
    return jnp.einsum('bhqk,bkhe->bqhe', w, v)


def chunk_causal_mask(q_pos, k_pos):
    return (k_pos[None, :] // CHUNK) <= (q_pos[:, None] // CHUNK)


def fox_attn(q, k, v, cum_q, cum_k, q_pos, k_pos):
    bias = jnp.swapaxes(cum_q, 1, 2)[..., :, None] - jnp.swapaxes(cum_k, 1, 2)[..., None, :]
    mask = k_pos[None, :] <= q_pos[:, None]
    return softmax_attn(q, k, v, bias, mask, ATT_SCALE)


def rel_bias_lookup(table, rel):
    return table[:, jnp.clip(rel, -REL_CLIP, REL_CLIP) + REL_CLIP]


def sweep_query_blocks(fn, q_args, k_args, seq):
    outs = []
    for b0 in range(0, seq, Q_BLOCK):
        b1 = min(b0 + Q_BLOCK, seq)
        outs.append(fn([a[:, b0:b1] for a in q_args], [a[:, :b1] for a in k_args],
                       jnp.arange(b0, b1), jnp.arange(b1)))
    return jnp.concatenate(outs, axis=1)


def ab_inputs(h, pos, w_in, lat_gain, w_uq, qk_gain):
    b, t, _ = h.shape
    cq, ckv, kr, sq, sk, sv = split_cols(h @ w_in, AB_SIZES)
    cq = rms_norm(cq, lat_gain[:Q_RANK])
    ckv = rms_norm(ckv, lat_gain[Q_RANK:])
    q = (cq @ w_uq).reshape(b, t, H_MLA, QK_HEAD)
    q_nope = rms_norm(q[..., :NOPE_DIM], qk_gain[0, :NOPE_DIM])
    q_rope = rope(rms_norm(q[..., NOPE_DIM:], qk_gain[0, NOPE_DIM:]), pos)
    k_rope = rope(rms_norm(kr, qk_gain[1, NOPE_DIM:]), pos)
    sq, sk, sv = [a.reshape(b, t, H_SB, HEAD_DIM) for a in (sq, sk, sv)]
    return jnp.concatenate([q_nope, q_rope], -1), ckv, k_rope, sq, sk, sv


def mla_keys(ckv, k_rope, w_ukv, k_gain):
    b, t, _ = ckv.shape
    kv = (ckv @ w_ukv).reshape(b, t, H_MLA, NOPE_DIM + V_DIM)
    k_nope = rms_norm(kv[..., :NOPE_DIM], k_gain[:NOPE_DIM])
    k_pe = jnp.broadcast_to(k_rope[:, :, None, :], (b, t, H_MLA, ROPE_DIM))
    return jnp.concatenate([k_nope, k_pe], -1), kv[..., NOPE_DIM:]


def ab_mixer(h, pos, past, w_in, lat_gain, w_uq, w_ukv, qk_gain, w_out):
    b, t, _ = h.shape
    q, ckv, k_rope, sq, sk, sv = ab_inputs(h, pos, w_in, lat_gain, w_uq, qk_gain)
    if past is None:
        k, v = mla_keys(ckv, k_rope, w_ukv, qk_gain[1])
        o_mla = sweep_query_blocks(
            lambda qs, ks, qp, kp: softmax_attn(qs[0], ks[0], ks[1], None,
                                                chunk_causal_mask(qp, kp), MLA_SCALE),
            [q], [k, v], t)
        o_sb = sweep_query_blocks(
            lambda qs, ks, qp, kp: stick_breaking_attn(qs[0], ks[0], ks[1], qp, kp),
            [sq], [sk, sv], t)
    else:
        c_ckv, c_kr, c_sk, c_sv = past
        k_pos = jnp.arange(c_ckv.shape[1] + t)
        k, v = mla_keys(jnp.concatenate([c_ckv, ckv], 1), jnp.concatenate([c_kr, k_rope], 1),
                        w_ukv, qk_gain[1])
        o_mla = softmax_attn(q, k, v, None, chunk_causal_mask(pos, k_pos), MLA_SCALE)
        o_sb = stick_breaking_attn(sq, jnp.concatenate([c_sk, sk], 1),
                                   jnp.concatenate([c_sv, sv], 1), pos, k_pos)
    o = jnp.concatenate([o_mla.reshape(b, t, -1), o_sb.reshape(b, t, -1)], -1) @ w_out
    return o, (ckv, k_rope, sk, sv)


def cd_inputs(h, w_in, f_bias, qk_gain):
    b, t, _ = h.shape
    bq, bk, bv, fq, fk, fv, fl = split_cols(h @ w_in, CD_SIZES)
    heads = lambda a: a.reshape(b, t, -1, HEAD_DIM)
    bq = rms_norm(heads(bq), qk_gain[0])
    bk = rms_norm(heads(bk), qk_gain[1])
    fq = rms_norm(heads(fq), qk_gain[2])
    fk = rms_norm(heads(fk), qk_gain[3])
    log_f = jax.nn.log_sigmoid(fl.astype(jnp.float32) + f_bias.astype(jnp.float32))
    return bq, bk, heads(bv), fq, fk, heads(fv), log_f


def band_attn_prompt(q, k, v, table):
    b, t, h, d = q.shape
    n = t // CHUNK
    w = BAND_LEFT + 1
    pad = ((0, 0), (BAND_LEFT, 0), (0, 0), (0, 0), (0, 0))
    idx = jnp.arange(n)[:, None] + jnp.arange(w)[None, :]
    kb = jnp.pad(k.reshape(b, n, CHUNK, h, d), pad)[:, idx].reshape(b, n, w * CHUNK, h, d)
    vb = jnp.pad(v.reshape(b, n, CHUNK, h, d), pad)[:, idx].reshape(b, n, w * CHUNK, h, d)
    valid = jnp.repeat(idx >= BAND_LEFT, CHUNK, axis=1)[:, None, None, :]
    rel = BAND_LEFT * CHUNK + jnp.arange(CHUNK)[:, None] - jnp.arange(w * CHUNK)[None, :]
    bias = rel_bias_lookup(table, rel)
    o = softmax_attn(q.reshape(b, n, CHUNK, h, d), kb, vb, bias, valid, ATT_SCALE)
    return o.reshape(b, t, h, d)


def band_attn_step(q, k, v, q_pos, k_pos, table):
    qc, kc = q_pos[:, None] // CHUNK, k_pos[None, :] // CHUNK
    mask = (kc <= qc) & (kc >= qc - BAND_LEFT)
    bias = rel_bias_lookup(table, q_pos[:, None] - k_pos[None, :])
    return softmax_attn(q, k, v, bias, mask, ATT_SCALE)


def cd_mixer(h, pos, past, band_len, w_in, f_bias, qk_gain, rel_table, w_out):
    b, t, _ = h.shape
    bq, bk, bv, fq, fk, fv, log_f = cd_inputs(h, w_in, f_bias, qk_gain)
    if past is None:
        o_band = band_attn_prompt(bq, bk, bv, rel_table)
        buf_k, buf_v = tail_rows(bk, band_len), tail_rows(bv, band_len)
        cum_f = jnp.cumsum(log_f, axis=1)
        o_fox = sweep_query_blocks(
            lambda qs, ks, qp, kp: fox_attn(qs[0], ks[0], ks[1], qs[1], ks[2], qp, kp),
            [fq, cum_f], [fk, fv, cum_f], t)
    else:
        c_bk, c_bv, c_fk, c_fv, c_lf = past
        p_len = c_fk.shape[1]
        kb = jnp.concatenate([c_bk, bk], 1)
        vb = jnp.concatenate([c_bv, bv], 1)
        kb_pos = jnp.arange(p_len - c_bk.shape[1], p_len + t)
        o_band = band_attn_step(bq, kb, vb, pos, kb_pos, rel_table)
        buf_k, buf_v = tail_rows(kb, band_len), tail_rows(vb, band_len)
        cum_f = jnp.cumsum(jnp.concatenate([c_lf.astype(jnp.float32), log_f], 1), axis=1)
        o_fox = fox_attn(fq, jnp.concatenate([c_fk, fk], 1), jnp.concatenate([c_fv, fv], 1),
                         cum_f[:, p_len:], cum_f, pos, jnp.arange(p_len + t))
    o = jnp.concatenate([o_band.reshape(b, t, -1), o_fox.reshape(b, t, -1)], -1) @ w_out
    return o, (buf_k, buf_v, fk, fv, log_f)


def run_trunk(x, past_len, caches, band_len, p):
    t = x.shape[1]
    pos = jnp.arange(past_len, past_len + t)
    ab_states, cd_states = [], []
    for l in range(DEPTH):
        g = p['norm_gain'][l]
        i = l // 2
        x = x + 0.5 * swiglu_ffn(x, g[0], p['ffn_w_gate'][l, 0], p['ffn_w_up'][l, 0],
                                 p['ffn_w_down'][l, 0])
        h = rms_norm(x, g[1])
        if l % 2 == 0:
            past = None if caches is None else tuple(c[i] for c in caches[:4])
            o, st = ab_mixer(h, pos, past, p['ab_w_in'][i], p['mla_lat_gain'][i],
                             p['mla_w_uq'][i], p['mla_w_ukv'][i], p['mla_qk_gain'][i],
                             p['ab_w_out'][i])
            ab_states.append(st)
        else:
            past = None if caches is None else tuple(c[i] for c in caches[4:])
            o, st = cd_mixer(h, pos, past, band_len, p['cd_w_in'][i], p['fox_f_bias'][i],
                             p['cd_qk_gain'][i], p['band_rel_bias'][i], p['cd_w_out'][i])
            cd_states.append(st)
        x = x + o
        x = x + 0.5 * swiglu_ffn(x, g[2], p['ffn_w_gate'][l, 1], p['ffn_w_up'][l, 1],
                                 p['ffn_w_down'][l, 1])
    states = [jnp.stack(s, 0) for s in zip(*ab_states)] + [jnp.stack(s, 0) for s in zip(*cd_states)]
    return x, states


def setup_inputs(seed: int = 0) -> dict:
    key = jax.random.key(seed)
    ks = iter(jax.random.split(key, 32))
    nrm = lambda shape, scale=1.0: scale * jax.random.normal(next(ks), shape, jnp.float32)
    band_len = min(BAND_LEFT * CHUNK, PAST_LEN)
    return {
        'x_prompt': nrm((BATCH, SEQ, D_MODEL)),
        'x_sample': nrm((DEC_BATCH, DEC_SEQ, D_MODEL)),
        'cache_mla_ckv': nrm((N_AB, DEC_BATCH, PAST_LEN, KV_RANK)),
        'cache_mla_krope': nrm((N_AB, DEC_BATCH, PAST_LEN, ROPE_DIM)),
        'cache_sb_k': nrm((N_AB, DEC_BATCH, PAST_LEN, H_SB, HEAD_DIM)),
        'cache_sb_v': nrm((N_AB, DEC_BATCH, PAST_LEN, H_SB, HEAD_DIM)),
        'cache_band_k': nrm((N_CD, DEC_BATCH, band_len, H_BAND, HEAD_DIM)),
        'cache_band_v': nrm((N_CD, DEC_BATCH, band_len, H_BAND, HEAD_DIM)),
        'cache_fox_k': nrm((N_CD, DEC_BATCH, PAST_LEN, H_FOX, HEAD_DIM)),
        'cache_fox_v': nrm((N_CD, DEC_BATCH, PAST_LEN, H_FOX, HEAD_DIM)),
        'cache_fox_logf': jax.nn.log_sigmoid(F_BIAS_INIT + nrm((N_CD, DEC_BATCH, PAST_LEN, H_FOX))),
        'norm_gain': 1.0 + nrm((DEPTH, 3, D_MODEL), 0.02),
        'ffn_w_gate': nrm((DEPTH, 2, D_MODEL, D_FF), D_MODEL ** -0.5),
        'ffn_w_up': nrm((DEPTH, 2, D_MODEL, D_FF), D_MODEL ** -0.5),
        'ffn_w_down': nrm((DEPTH, 2, D_FF, D_MODEL), D_FF ** -0.5),
        'ab_w_in': nrm((N_AB, D_MODEL, AB_IN), D_MODEL ** -0.5),
        'mla_lat_gain': 1.0 + nrm((N_AB, Q_RANK + KV_RANK), 0.02),
        'mla_w_uq': nrm((N_AB, Q_RANK, H_MLA * QK_HEAD), Q_RANK ** -0.5),
        'mla_w_ukv': nrm((N_AB, KV_RANK, H_MLA * (NOPE_DIM + V_DIM)), KV_RANK ** -0.5),
        'mla_qk_gain': 1.0 + nrm((N_AB, 2, QK_HEAD), 0.02),
        'ab_w_out': nrm((N_AB, AB_OUT, D_MODEL), AB_OUT ** -0.5),
        'cd_w_in': nrm((N_CD, D_MODEL, CD_IN), D_MODEL ** -0.5),
        'fox_f_bias': F_BIAS_INIT + nrm((N_CD, H_FOX), 0.1),
        'cd_qk_gain': 1.0 + nrm((N_CD, 4, HEAD_DIM), 0.02),
        'band_rel_bias': nrm((N_CD, H_BAND, 2 * REL_CLIP + 1), 0.1),
        'cd_w_out': nrm((N_CD, CD_OUT, D_MODEL), CD_OUT ** -0.5),
    }


def reference(x_prompt, x_sample, cache_mla_ckv, cache_mla_krope, cache_sb_k, cache_sb_v,
              cache_band_k, cache_band_v, cache_fox_k, cache_fox_v, cache_fox_logf,
              norm_gain, ffn_w_gate, ffn_w_up, ffn_w_down, ab_w_in, mla_lat_gain, mla_w_uq,
              mla_w_ukv, mla_qk_gain, ab_w_out, cd_w_in, fox_f_bias, cd_qk_gain, band_rel_bias,
              cd_w_out):
    p = dict(norm_gain=norm_gain, ffn_w_gate=ffn_w_gate, ffn_w_up=ffn_w_up, ffn_w_down=ffn_w_down,
             ab_w_in=ab_w_in, mla_lat_gain=mla_lat_gain, mla_w_uq=mla_w_uq, mla_w_ukv=mla_w_ukv,
             mla_qk_gain=mla_qk_gain, ab_w_out=ab_w_out, cd_w_in=cd_w_in, fox_f_bias=fox_f_bias,
             cd_qk_gain=cd_qk_gain, band_rel_bias=band_rel_bias, cd_w_out=cd_w_out)
    band_len = cache_band_k.shape[2]
    caches = (cache_mla_ckv, cache_mla_krope, cache_sb_k, cache_sb_v,
              cache_band_k, cache_band_v, cache_fox_k, cache_fox_v, cache_fox_logf)
    y_prompt, states_p = run_trunk(x_prompt, 0, None, band_len, p)
    y_sample, states_s = run_trunk(x_sample, cache_fox_k.shape[2], caches, band_len, p)
    (ckv_p, kr_p, sbk_p, sbv_p, bk_p, bv_p, fk_p, fv_p, lf_p) = states_p
    (ckv_s, kr_s, sbk_s, sbv_s, bk_s, bv_s, fk_s, fv_s, lf_s) = states_s
    return (y_prompt, y_sample, ckv_p, ckv_s, kr_p, kr_s, sbk_p, sbk_s, sbv_p, sbv_s,
            bk_p, bk_s, bv_p, bv_s, fk_p, fk_s, fv_p, fv_s, lf_p, lf_s)
```

```python
import functools

import jax
import jax.numpy as jnp
from jax import lax
from jax.experimental import pallas as pl
from jax.experimental.pallas import tpu as pltpu

F32 = jnp.float32
BF16 = jnp.bfloat16

EPS = 1e-6
NEG_INF = -1e30
CHUNK = 64
BAND_LEFT = 8
REL_CLIP = 128
N_HEADS = 8
HEAD_DIM = 128
Q_RANK = 512
KV_RANK = 256
ROPE_DIM = 64
NOPE_DIM = 128
QK_HEAD = NOPE_DIM + ROPE_DIM
MLA_QK_PAD = 256
ROPE_THETA = 10000.0
MLA_SCALE = QK_HEAD ** -0.5
ATT_SCALE = HEAD_DIM ** -0.5
HW = N_HEADS * HEAD_DIM
AB_COLS = 4096
CD_COLS = 6400
BAND_KEYS = (BAND_LEFT + 1) * CHUNK

VMEM_LIMIT_BYTES = 48 * 1024 * 1024


def _params(*sem):
    return pltpu.CompilerParams(dimension_semantics=sem, vmem_limit_bytes=VMEM_LIMIT_BYTES)


def _rms(x, gain, inv_n=None):
    if inv_n is None:
        ms = jnp.mean(x * x, axis=-1, keepdims=True)
    else:
        ms = jnp.sum(x * x, axis=-1, keepdims=True) * inv_n
    return x * lax.rsqrt(ms + EPS) * gain


def _dot(a, b):
    return jnp.dot(a, b, preferred_element_type=F32)


def _dot_t(a, b):
    return lax.dot_general(a, b, (((1,), (1,)), ((), ())), preferred_element_type=F32)


def _ffn_kernel(x_ref, g_ref, wg_ref, wu_ref, wd_ref, o_ref, h_ref, *, n_f):
    j = pl.program_id(1)

    @pl.when(j == 0)
    def _():
        h_ref[...] = _rms(x_ref[...], g_ref[...]).astype(BF16)

    h = h_ref[...]
    g = _dot(h, wg_ref[...])
    u = _dot(h, wu_ref[...])
    a = (g / (1.0 + jnp.exp(-g)) * u).astype(BF16)
    d = _dot(a, wd_ref[...])

    @pl.when(j == 0)
    def _():
        o_ref[...] = d

    @pl.when(j > 0)
    def _():
        o_ref[...] += d

    @pl.when(j == n_f - 1)
    def _():
        o_ref[...] = x_ref[...] + 0.5 * o_ref[...]


def _ffn(x, gain, wg, wu, wd):
    n, d = x.shape
    f = wg.shape[1]
    tm = min(512, n)
    tf = 512 if f % 512 == 0 else f
    n_f = f // tf
    return pl.pallas_call(
        functools.partial(_ffn_kernel, n_f=n_f),
        grid=(n // tm, n_f),
        in_specs=[
            pl.BlockSpec((tm, d), lambda i, j: (i, 0)),
            pl.BlockSpec((1, d), lambda i, j: (0, 0)),
            pl.BlockSpec((d, tf), lambda i, j: (0, j)),
            pl.BlockSpec((d, tf), lambda i, j: (0, j)),
            pl.BlockSpec((tf, d), lambda i, j: (j, 0)),
        ],
        out_specs=pl.BlockSpec((tm, d), lambda i, j: (i, 0)),
        out_shape=jax.ShapeDtypeStruct((n, d), F32),
        scratch_shapes=[pltpu.VMEM((tm, d), BF16)],
        compiler_params=_params("parallel", "arbitrary"),
        name="ffn",
    )(x, gain, wg, wu, wd)


def _proj_in_kernel(x_ref, g_ref, w_ref, o_ref, h_ref):
    @pl.when(pl.program_id(1) == 0)
    def _():
        h_ref[...] = _rms(x_ref[...], g_ref[...]).astype(BF16)

    o_ref[...] = _dot(h_ref[...], w_ref[...])


def _proj_in(x, gain, w, tn):
    n, d = x.shape
    c = w.shape[1]
    tm = min(512, n)
    return pl.pallas_call(
        _proj_in_kernel,
        grid=(n // tm, c // tn),
        in_specs=[
            pl.BlockSpec((tm, d), lambda i, j: (i, 0)),
            pl.BlockSpec((1, d), lambda i, j: (0, 0)),
            pl.BlockSpec((d, tn), lambda i, j: (0, j)),
        ],
        out_specs=pl.BlockSpec((tm, tn), lambda i, j: (i, j)),
        out_shape=jax.ShapeDtypeStruct((n, c), F32),
        scratch_shapes=[pltpu.VMEM((tm, d), BF16)],
        compiler_params=_params("parallel", "arbitrary"),
        name="proj_in",
    )(x, gain, w)


def _proj_out_kernel(a_ref, b_ref, wa_ref, wb_ref, x_ref, o_ref):
    o_ref[...] = x_ref[...] + _dot(a_ref[...], wa_ref[...]) + _dot(b_ref[...], wb_ref[...])


def _proj_out(oa, ob, wa, wb, x):
    n, d = x.shape
    k = oa.shape[1]
    tm = min(512, n)
    tn = min(1024, d)
    return pl.pallas_call(
        _proj_out_kernel,
        grid=(n // tm, d // tn),
        in_specs=[
            pl.BlockSpec((tm, k), lambda i, j: (i, 0)),
            pl.BlockSpec((tm, k), lambda i, j: (i, 0)),
            pl.BlockSpec((k, tn), lambda i, j: (0, j)),
            pl.BlockSpec((k, tn), lambda i, j: (0, j)),
            pl.BlockSpec((tm, tn), lambda i, j: (i, j)),
        ],
        out_specs=pl.BlockSpec((tm, tn), lambda i, j: (i, j)),
        out_shape=jax.ShapeDtypeStruct((n, d), F32),
        compiler_params=_params("parallel", "parallel"),
        name="proj_out",
    )(oa, ob, wa, wb, x)


def _rope_slab(x, cs):
    c = cs[:, 0:128]
    s1 = cs[:, 128:256]
    s2 = cs[:, 256:384]
    return x * c + pltpu.roll(x, 96, 1) * s1 + pltpu.roll(x, 32, 1) * s2


def _ab_post_kernel(sq_ref, sk_ref, sv_ref, tail_ref, cs_ref, gn_ref, wuq_ref,
                    q_ref, ckv_ref, kr_ref, sqb_ref, skb_ref, svb_ref, skf_ref, svf_ref):
    sqb_ref[...] = (sq_ref[...] * ATT_SCALE).astype(BF16)
    sk = sk_ref[...]
    sv = sv_ref[...]
    skf_ref[...] = sk
    svf_ref[...] = sv
    skb_ref[...] = sk.astype(BF16)
    svb_ref[...] = sv.astype(BF16)

    cs = cs_ref[...]
    g_cq = gn_ref[0:1, :]
    g_ckv = gn_ref[1:2, 0:KV_RANK]
    gq_n = gn_ref[2:3, 0:128]
    gq_r = gn_ref[3:4, 0:128]
    gk_r = gn_ref[4:5, 0:128]

    cq = _rms(tail_ref[:, 0:Q_RANK], g_cq)
    ckv_ref[...] = _rms(tail_ref[:, Q_RANK:Q_RANK + KV_RANK], g_ckv)
    kr = _rms(tail_ref[:, 768:896], gk_r, 1.0 / ROPE_DIM)
    kr_ref[...] = _rope_slab(kr, cs)[:, 0:ROPE_DIM]

    q = _dot(cq.astype(BF16), wuq_ref[...])
    for h in range(N_HEADS):
        lo = h * MLA_QK_PAD
        qn = _rms(q[:, lo:lo + 128], gq_n)
        qr = _rope_slab(_rms(q[:, lo + 128:lo + 256], gq_r, 1.0 / ROPE_DIM), cs)
        q_ref[:, lo:lo + 128] = (qn * MLA_SCALE).astype(BF16)
        q_ref[:, lo + 128:lo + 256] = (qr * MLA_SCALE).astype(BF16)


def _ab_post(proj, cs_tab, gains, wuq, t):
    n = proj.shape[0]
    tm = min(256, n, cs_tab.shape[0])
    n_tab = cs_tab.shape[0] // tm
    row = lambda i: (i, 0)
    return pl.pallas_call(
        _ab_post_kernel,
        grid=(n // tm,),
        in_specs=[
            pl.BlockSpec((tm, HW), lambda i: (i, 0)),
            pl.BlockSpec((tm, HW), lambda i: (i, 1)),
            pl.BlockSpec((tm, HW), lambda i: (i, 2)),
            pl.BlockSpec((tm, HW), lambda i: (i, 3)),
            pl.BlockSpec((tm, 384), lambda i: (i % n_tab, 0)),
            pl.BlockSpec(gains.shape, lambda i: (0, 0)),
            pl.BlockSpec(wuq.shape, lambda i: (0, 0)),
        ],
        out_specs=[
            pl.BlockSpec((tm, N_HEADS * MLA_QK_PAD), row),
            pl.BlockSpec((tm, KV_RANK), row),
            pl.BlockSpec((tm, ROPE_DIM), row),
            pl.BlockSpec((tm, HW), row),
            pl.BlockSpec((tm, HW), row),
            pl.BlockSpec((tm, HW), row),
            pl.BlockSpec((tm, HW), row),
            pl.BlockSpec((tm, HW), row),
        ],
        out_shape=[
            jax.ShapeDtypeStruct((n, N_HEADS * MLA_QK_PAD), BF16),
            jax.ShapeDtypeStruct((n, KV_RANK), F32),
            jax.ShapeDtypeStruct((n, ROPE_DIM), F32),
            jax.ShapeDtypeStruct((n, HW), BF16),
            jax.ShapeDtypeStruct((n, HW), BF16),
            jax.ShapeDtypeStruct((n, HW), BF16),
            jax.ShapeDtypeStruct((n, HW), F32),
            jax.ShapeDtypeStruct((n, HW), F32),
        ],
        compiler_params=_params("parallel"),
        name="ab_post",
    )(proj, proj, proj, proj, cs_tab, gains, wuq)


def _mla_kv_kernel(ckv_ref, kr_ref, w_ref, g_ref, k_ref, v_ref):
    kv = _dot(ckv_ref[...].astype(BF16), w_ref[...])
    m = kv.shape[0]
    kr = jnp.concatenate([kr_ref[...], jnp.zeros((m, 128 - ROPE_DIM), F32)], axis=1).astype(BF16)
    g = g_ref[...]
    for h in range(N_HEADS):
        lo = h * 256
        k_ref[:, lo:lo + 128] = _rms(kv[:, lo:lo + 128], g).astype(BF16)
        k_ref[:, lo + 128:lo + 256] = kr
        v_ref[:, h * 128:(h + 1) * 128] = kv[:, lo + 128:lo + 256].astype(BF16)


def _mla_kv(ckv, kr, w_ukv, gk_n):
    m = ckv.shape[0]
    tm = min(512, m)
    row = lambda i: (i, 0)
    return pl.pallas_call(
        _mla_kv_kernel,
        grid=(m // tm,),
        in_specs=[
            pl.BlockSpec((tm, KV_RANK), row),
            pl.BlockSpec((tm, ROPE_DIM), row),
            pl.BlockSpec(w_ukv.shape, lambda i: (0, 0)),
            pl.BlockSpec((1, 128), lambda i: (0, 0)),
        ],
        out_specs=[
            pl.BlockSpec((tm, N_HEADS * MLA_QK_PAD), row),
            pl.BlockSpec((tm, HW), row),
        ],
        out_shape=[
            jax.ShapeDtypeStruct((m, N_HEADS * MLA_QK_PAD), BF16),
            jax.ShapeDtypeStruct((m, HW), BF16),
        ],
        compiler_params=_params("parallel"),
        name="mla_kv",
    )(ckv, kr, w_ukv, gk_n)


def _cd_post_kernel(bq_ref, bk_ref, bv_ref, fq_ref, fk_ref, fv_ref, fl_ref, gn_ref, fb_ref,
                    bqb_ref, bkf_ref, bkb_ref, bvf_ref, bvb_ref,
                    fqb_ref, fkf_ref, fkb_ref, fvf_ref, fvb_ref, lf_ref):
    bv = bv_ref[...]
    bvf_ref[...] = bv
    bvb_ref[...] = bv.astype(BF16)
    fv = fv_ref[...]
    fvf_ref[...] = fv
    fvb_ref[...] = fv.astype(BF16)
    for h in range(N_HEADS):
        sl = slice(h * 128, (h + 1) * 128)
        bqb_ref[:, sl] = (_rms(bq_ref[:, sl], gn_ref[0:1, :]) * ATT_SCALE).astype(BF16)
        bk = _rms(bk_ref[:, sl], gn_ref[1:2, :])
        bkf_ref[:, sl] = bk
        bkb_ref[:, sl] = bk.astype(BF16)
        fqb_ref[:, sl] = (_rms(fq_ref[:, sl], gn_ref[2:3, :]) * ATT_SCALE).astype(BF16)
        fk = _rms(fk_ref[:, sl], gn_ref[3:4, :])
        fkf_ref[:, sl] = fk
        fkb_ref[:, sl] = fk.astype(BF16)
    z = fl_ref[:, 0:128] + fb_ref[...]
    lf = -(jnp.maximum(-z, 0.0) + jnp.log1p(jnp.exp(-jnp.abs(z))))
    lf_ref[...] = lf[:, 0:N_HEADS]


def _cd_post(proj, gains, f_bias):
    n = proj.shape[0]
    tm = min(256, n)
    row = lambda i: (i, 0)
    col = lambda c: pl.BlockSpec((tm, HW), lambda i, c=c: (i, c))
    f32o = jax.ShapeDtypeStruct((n, HW), F32)
    b16o = jax.ShapeDtypeStruct((n, HW), BF16)
    blk = pl.BlockSpec((tm, HW), row)
    return pl.pallas_call(
        _cd_post_kernel,
        grid=(n // tm,),
        in_specs=[col(0), col(1), col(2), col(3), col(4), col(5),
                  pl.BlockSpec((tm, 256), lambda i: (i, 6 * HW // 256)),
                  pl.BlockSpec((4, 128), lambda i: (0, 0)),
                  pl.BlockSpec((1, 128), lambda i: (0, 0))],
        out_specs=[blk] * 10 + [pl.BlockSpec((tm, N_HEADS), row)],
        out_shape=[b16o, f32o, b16o, f32o, b16o, b16o, f32o, b16o, f32o, b16o,
                   jax.ShapeDtypeStruct((n, N_HEADS), F32)],
        compiler_params=_params("parallel"),
        name="cd_post",
    )(proj, proj, proj, proj, proj, proj, proj, gains, f_bias)


def _cumsum_kernel(x_ref, o_ref, *, n_blk):
    jj = lax.broadcasted_iota(jnp.int32, (128, 128), 0)
    ss = lax.broadcasted_iota(jnp.int32, (128, 128), 1)
    upper = jnp.where(jj <= ss, 1.0, 0.0).astype(BF16)

    def body(i, carry):
        x = x_ref[i]
        hi = x.astype(BF16)
        r1 = x - hi.astype(F32)
        mid = r1.astype(BF16)
        lo = (r1 - mid.astype(F32)).astype(BF16)
        c = _dot(hi, upper) + _dot(mid, upper) + _dot(lo, upper) + carry
        o_ref[i] = c
        return c[:, 127:128]

    lax.fori_loop(0, n_blk, body, jnp.zeros((N_HEADS, 1), F32))


def _cumsum_rows(lf):
    b, length, nh = lf.shape
    n_blk = -(-length // 128)
    x = jnp.pad(jnp.swapaxes(lf, 1, 2), ((0, 0), (0, 0), (0, n_blk * 128 - length)))
    x = jnp.swapaxes(x.reshape(b, nh, n_blk, 128), 1, 2)
    out = pl.pallas_call(
        functools.partial(_cumsum_kernel, n_blk=n_blk),
        grid=(b,),
        in_specs=[pl.BlockSpec((None, n_blk, nh, 128), lambda i: (i, 0, 0, 0))],
        out_specs=pl.BlockSpec((None, n_blk, nh, 128), lambda i: (i, 0, 0, 0)),
        out_shape=jax.ShapeDtypeStruct((b, n_blk, nh, 128), F32),
        compiler_params=_params("parallel"),
        name="cumsum",
    )(x)
    return jnp.swapaxes(out, 1, 2).reshape(b, nh, n_blk * 128)[:, :, :length]


def _rel_bias_kernel(tab_ref, o_ref, *, n_tab):
    h = pl.program_id(0)
    i = lax.broadcasted_iota(jnp.int32, (CHUNK, BAND_KEYS), 0)
    j = lax.broadcasted_iota(jnp.int32, (CHUNK, BAND_KEYS), 1)
    idx = jnp.clip(BAND_LEFT * CHUNK + i - j, -REL_CLIP, REL_CLIP) + REL_CLIP

    def body(r, acc):
        return jnp.where(idx == r, tab_ref[h, r], acc)

    o_ref[...] = lax.fori_loop(0, n_tab, body, jnp.zeros((CHUNK, BAND_KEYS), F32))


def _rel_bias(table):
    nh, n_tab = table.shape
    return pl.pallas_call(
        functools.partial(_rel_bias_kernel, n_tab=n_tab),
        grid=(nh,),
        in_specs=[pl.BlockSpec(memory_space=pltpu.SMEM)],
        out_specs=pl.BlockSpec((None, CHUNK, BAND_KEYS), lambda h: (h, 0, 0)),
        out_shape=jax.ShapeDtypeStruct((nh, CHUNK, BAND_KEYS), F32),
        compiler_params=_params("arbitrary"),
        name="rel_bias",
    )(table)


def _flash_kernel(*refs, tq, mode, has_bias):
    if has_bias:
        q_ref, k_ref, v_ref, cq_ref, ck_ref, o_ref = refs
    else:
        q_ref, k_ref, v_ref, o_ref = refs
    h = pl.program_id(1)
    qi = pl.program_id(2)
    q = q_ref[...]
    if has_bias:
        lane = lax.broadcasted_iota(jnp.int32, cq_ref.shape, 1)
        cq = jnp.sum(jnp.where(lane == h, cq_ref[...], 0.0), axis=-1, keepdims=True)

    def step(kb, carry, masked):
        m, l, acc = carry
        ks = pl.multiple_of(kb * tq, tq)
        s = _dot_t(q, k_ref[pl.ds(ks, tq), :])
        if has_bias:
            s = s + (cq - ck_ref[pl.ds(kb, 1), :])
        if masked:
            row = lax.broadcasted_iota(jnp.int32, (tq, tq), 0)
            col = lax.broadcasted_iota(jnp.int32, (tq, tq), 1)
            if mode == "chunk":
                ok = (col // CHUNK) <= (row // CHUNK)
            else:
                ok = col <= row
            s = jnp.where(ok, s, NEG_INF)
        m_new = jnp.maximum(m, jnp.max(s, axis=-1, keepdims=True))
        alpha = jnp.exp(m - m_new)
        p = jnp.exp(s - m_new)
        l = alpha * l + jnp.sum(p, axis=-1, keepdims=True)
        acc = alpha * acc + _dot(p.astype(BF16), v_ref[pl.ds(ks, tq), :])
        return m_new, l, acc

    init = (jnp.full((tq, 1), NEG_INF, F32), jnp.zeros((tq, 1), F32), jnp.zeros((tq, HEAD_DIM), F32))
    carry = lax.fori_loop(0, qi, lambda kb, c: step(kb, c, False), init)
    m, l, acc = step(qi, carry, True)
    o_ref[...] = (acc / l).astype(BF16)


def _flash(q, k, v, mode, cq=None, ck=None):
    b, t, _ = q.shape
    dk = q.shape[2] // N_HEADS
    tq = min(256, t)
    nq = t // tq
    has_bias = cq is not None
    in_specs = [
        pl.BlockSpec((None, tq, dk), lambda bi, h, qi: (bi, qi, h)),
        pl.BlockSpec((None, t, dk), lambda bi, h, qi: (bi, 0, h)),
        pl.BlockSpec((None, t, HEAD_DIM), lambda bi, h, qi: (bi, 0, h)),
    ]
    args = [q, k, v]
    if has_bias:
        in_specs += [
            pl.BlockSpec((None, tq, N_HEADS), lambda bi, h, qi: (bi, qi, 0)),
            pl.BlockSpec((None, None, nq, tq), lambda bi, h, qi: (bi, h, 0, 0)),
        ]
        args += [cq, ck.reshape(b, N_HEADS, nq, tq)]
    return pl.pallas_call(
        functools.partial(_flash_kernel, tq=tq, mode=mode, has_bias=has_bias),
        grid=(b, N_HEADS, nq),
        in_specs=in_specs,
        out_specs=pl.BlockSpec((None, tq, HEAD_DIM), lambda bi, h, qi: (bi, qi, h)),
        out_shape=jax.ShapeDtypeStruct((b, t, HW), BF16),
        compiler_params=_params("parallel", "parallel", "arbitrary"),
        name="flash_" + mode,
    )(*args)


def _sb_block(z, c, v, tri, ok):
    sp = jnp.maximum(z, 0.0) + jnp.log1p(jnp.exp(-jnp.abs(z)))
    lk = -sp if ok is None else jnp.where(ok, -sp, 0.0)
    hi = lk.astype(BF16)
    lo = (lk - hi.astype(F32)).astype(BF16)
    between = _dot(hi, tri) + _dot(lo, tri) + c
    w = jnp.exp((z - sp) + between)
    if ok is not None:
        w = jnp.where(ok, w, 0.0)
    return _dot(w.astype(BF16), v), c + jnp.sum(lk, axis=-1, keepdims=True)


def _tri(n):
    jj = lax.broadcasted_iota(jnp.int32, (n, n), 0)
    ss = lax.broadcasted_iota(jnp.int32, (n, n), 1)
    return jnp.where(jj > ss, 1.0, 0.0).astype(BF16)


def _sb_kernel(q_ref, k_ref, v_ref, o_ref, *, tq):
    qi = pl.program_id(2)
    q = q_ref[...]
    tri = _tri(tq)

    def step(kb, carry, masked):
        c, acc = carry
        ks = pl.multiple_of(kb * tq, tq)
        z = _dot_t(q, k_ref[pl.ds(ks, tq), :])
        ok = None
        if masked:
            row = lax.broadcasted_iota(jnp.int32, (tq, tq), 0)
            col = lax.broadcasted_iota(jnp.int32, (tq, tq), 1)
            ok = col < row
        o, c = _sb_block(z, c, v_ref[pl.ds(ks, tq), :], tri, ok)
        return c, acc + o

    carry = step(qi, (jnp.zeros((tq, 1), F32), jnp.zeros((tq, HEAD_DIM), F32)), True)
    _, acc = lax.fori_loop(0, qi, lambda i, cr: step(qi - 1 - i, cr, False), carry)
    o_ref[...] = acc.astype(BF16)


def _sb_prompt(q, k, v):
    b, t, _ = q.shape
    tq = min(256, t)
    return pl.pallas_call(
        functools.partial(_sb_kernel, tq=tq),
        grid=(b, N_HEADS, t // tq),
        in_specs=[
            pl.BlockSpec((None, tq, HEAD_DIM), lambda bi, h, qi: (bi, qi, h)),
            pl.BlockSpec((None, t, HEAD_DIM), lambda bi, h, qi: (bi, 0, h)),
            pl.BlockSpec((None, t, HEAD_DIM), lambda bi, h, qi: (bi, 0, h)),
        ],
        out_specs=pl.BlockSpec((None, tq, HEAD_DIM), lambda bi, h, qi: (bi, qi, h)),
        out_shape=jax.ShapeDtypeStruct((b, t, HW), BF16),
        compiler_params=_params("parallel", "parallel", "arbitrary"),
        name="sb_prompt",
    )(q, k, v)


def _band_kernel(q_ref, k_ref, v_ref, b_ref, o_ref, *, n_chunks):
    bias = b_ref[...]
    col = lax.broadcasted_iota(jnp.int32, (CHUNK, BAND_KEYS), 1)

    def body(n, _):
        qs = pl.multiple_of(n * CHUNK, CHUNK)
        s = _dot_t(q_ref[pl.ds(qs, CHUNK), :], k_ref[pl.ds(qs, BAND_KEYS), :]) + bias
        s = jnp.where(col >= (BAND_LEFT - n) * CHUNK, s, NEG_INF)
        p = jnp.exp(s - jnp.max(s, axis=-1, keepdims=True))
        o = _dot(p.astype(BF16), v_ref[pl.ds(qs, BAND_KEYS), :]) / jnp.sum(p, axis=-1, keepdims=True)
        o_ref[pl.ds(qs, CHUNK), :] = o.astype(BF16)
        return 0

    lax.fori_loop(0, n_chunks, body, 0)


def _band_prompt(q, k, v, bias):
    b, t, _ = q.shape
    tp = k.shape[1]
    return pl.pallas_call(
        functools.partial(_band_kernel, n_chunks=t // CHUNK),
        grid=(b, N_HEADS),
        in_specs=[
            pl.BlockSpec((None, t, HEAD_DIM), lambda bi, h: (bi, 0, h)),
            pl.BlockSpec((None, tp, HEAD_DIM), lambda bi, h: (bi, 0, h)),
            pl.BlockSpec((None, tp, HEAD_DIM), lambda bi, h: (bi, 0, h)),
            pl.BlockSpec((None, CHUNK, BAND_KEYS), lambda bi, h: (h, 0, 0)),
        ],
        out_specs=pl.BlockSpec((None, t, HEAD_DIM), lambda bi, h: (bi, 0, h)),
        out_shape=jax.ShapeDtypeStruct((b, t, HW), BF16),
        compiler_params=_params("parallel", "parallel"),
        name="band_prompt",
    )(q, k, v, bias)


def _step_softmax_kernel(*refs, mode):
    if mode == "fox":
        q_ref, kp_ref, vp_ref, kn_ref, vn_ref, cq_ref, ckp_ref, ckn_ref, o_ref = refs
    elif mode == "band":
        q_ref, kp_ref, vp_ref, kn_ref, vn_ref, bp_ref, bn_ref, o_ref = refs
    else:
        q_ref, kp_ref, vp_ref, kn_ref, vn_ref, o_ref = refs
    t = q_ref.shape[0]
    dk = q_ref.shape[1] // N_HEADS
    for h in range(N_HEADS):
        ks = slice(h * dk, (h + 1) * dk)
        vs = slice(h * HEAD_DIM, (h + 1) * HEAD_DIM)
        q = q_ref[:, ks]
        sp = _dot_t(q, kp_ref[:, ks].astype(BF16))
        sn = _dot_t(q, kn_ref[:, ks])
        if mode == "fox":
            cq = cq_ref[:, h:h + 1]
            sp = sp + (cq - ckp_ref[h:h + 1, :])
            sn = sn + (cq - ckn_ref[h:h + 1, :])
            row = lax.broadcasted_iota(jnp.int32, (t, t), 0)
            col = lax.broadcasted_iota(jnp.int32, (t, t), 1)
            sn = jnp.where(col <= row, sn, NEG_INF)
        elif mode == "band":
            sp = sp + bp_ref[h]
            sn = sn + bn_ref[h]
        m = jnp.maximum(jnp.max(sp, axis=-1, keepdims=True), jnp.max(sn, axis=-1, keepdims=True))
        pp = jnp.exp(sp - m)
        pn = jnp.exp(sn - m)
        l = jnp.sum(pp, axis=-1, keepdims=True) + jnp.sum(pn, axis=-1, keepdims=True)
        o = _dot(pp.astype(BF16), vp_ref[:, vs].astype(BF16)) + _dot(pn.astype(BF16), vn_ref[:, vs])
        o_ref[:, vs] = (o / l).astype(BF16)


def _step_softmax(mode, q, kp, vp, kn, vn, *extra):
    b, t, qw = q.shape
    p = kp.shape[1]
    full = lambda a: pl.BlockSpec((None,) + a.shape[1:], lambda i: (i,) + (0,) * (a.ndim - 1))
    in_specs = [full(q), full(kp), full(vp), full(kn), full(vn)]
    if mode == "fox":
        in_specs += [full(e) for e in extra]
    elif mode == "band":
        in_specs += [pl.BlockSpec(e.shape, lambda i: (0, 0, 0)) for e in extra]
    return pl.pallas_call(
        functools.partial(_step_softmax_kernel, mode=mode),
        grid=(b,),
        in_specs=in_specs,
        out_specs=pl.BlockSpec((None, t, HW), lambda i: (i, 0, 0)),
        out_shape=jax.ShapeDtypeStruct((b, t, HW), BF16),
        compiler_params=_params("parallel"),
        name="step_" + mode,
    )(q, kp, vp, kn, vn, *extra)


def _step_sb_kernel(q_ref, kp_ref, vp_ref, kn_ref, vn_ref, o_ref, *, blk):
    t = q_ref.shape[0]
    p = kp_ref.shape[0]
    tri_n = _tri(t)
    tri_p = _tri(blk)
    row = lax.broadcasted_iota(jnp.int32, (t, t), 0)
    col = lax.broadcasted_iota(jnp.int32, (t, t), 1)
    for h in range(N_HEADS):
        hs = slice(h * HEAD_DIM, (h + 1) * HEAD_DIM)
        q = q_ref[:, hs]
        acc, c = _sb_block(_dot_t(q, kn_ref[:, hs]), jnp.zeros((t, 1), F32), vn_ref[:, hs], tri_n, col < row)
        for kb in reversed(range(p // blk)):
            rs = slice(kb * blk, (kb + 1) * blk)
            o, c = _sb_block(_dot_t(q, kp_ref[rs, hs].astype(BF16)), c, vp_ref[rs, hs].astype(BF16), tri_p, None)
            acc = acc + o
        o_ref[:, hs] = acc.astype(BF16)


def _step_sb(q, kp, vp, kn, vn):
    b, t, _ = q.shape
    p = kp.shape[1]
    blk = min(256, p)
    full = lambda a: pl.BlockSpec((None,) + a.shape[1:], lambda i: (i, 0, 0))
    return pl.pallas_call(
        functools.partial(_step_sb_kernel, blk=blk),
        grid=(b,),
        in_specs=[full(q), full(kp), full(vp), full(kn), full(vn)],
        out_specs=pl.BlockSpec((None, t, HW), lambda i: (i, 0, 0)),
        out_shape=jax.ShapeDtypeStruct((b, t, HW), BF16),
        compiler_params=_params("parallel"),
        name="step_sb",
    )(q, kp, vp, kn, vn)


def _rope_tables(pos, rows):
    half = ROPE_DIM // 2
    inv = ROPE_THETA ** (-jnp.arange(half, dtype=F32) / half)
    ang = pos.astype(F32)[:, None] * inv[None, :]
    c, s, z = jnp.cos(ang), jnp.sin(ang), jnp.zeros_like(ang)
    tab = jnp.concatenate([c, c, z, z, -s, z, z, z, z, s, z, z], axis=1)
    return jnp.tile(tab, (max(1, rows // tab.shape[0]), 1))


def _prep_ab(w_in, lat_gain, w_uq, w_ukv, qk_gain, w_out):
    d = w_in.shape[0]
    cq, ckv, kr, sq, sk, sv = jnp.split(w_in, [512, 768, 832, 1856, 2880], axis=1)
    w_in_p = jnp.concatenate([sq, sk, sv, cq, ckv, kr, jnp.zeros((d, AB_COLS - w_in.shape[1]), F32)], axis=1)
    uq = w_uq.reshape(Q_RANK, N_HEADS, QK_HEAD)
    uq = jnp.pad(uq, ((0, 0), (0, 0), (0, MLA_QK_PAD - QK_HEAD))).reshape(Q_RANK, N_HEADS * MLA_QK_PAD)
    gains = jnp.zeros((8, Q_RANK), F32)
    gains = gains.at[0, :].set(lat_gain[:Q_RANK])
    gains = gains.at[1, :KV_RANK].set(lat_gain[Q_RANK:])
    gains = gains.at[2, :NOPE_DIM].set(qk_gain[0, :NOPE_DIM])
    gains = gains.at[3, :ROPE_DIM].set(qk_gain[0, NOPE_DIM:])
    gains = gains.at[4, :ROPE_DIM].set(qk_gain[1, NOPE_DIM:])
    return dict(w_in=w_in_p.astype(BF16), w_uq=uq.astype(BF16), w_ukv=w_ukv.astype(BF16), gains=gains,
                gk_n=qk_gain[1:2, :NOPE_DIM], w_out_a=w_out[:HW].astype(BF16), w_out_b=w_out[HW:].astype(BF16))


def _prep_cd(w_in, f_bias, qk_gain, w_out):
    d = w_in.shape[0]
    w_in_p = jnp.pad(w_in, ((0, 0), (0, CD_COLS - w_in.shape[1])))
    fb = jnp.pad(f_bias, (0, 128 - N_HEADS)).reshape(1, 128)
    return dict(w_in=w_in_p.astype(BF16), gains=qk_gain, f_bias=fb,
                w_out_a=w_out[:HW].astype(BF16), w_out_b=w_out[HW:].astype(BF16))


def _ab_mixer(x, b, t, pos, past, p):
    n = x.shape[0]
    proj = _proj_in(x, p["g_mix"], p["w_in"], 1024)
    cs = _rope_tables(pos, min(256, n))
    q, ckv, kr, sq, skb, svb, skf, svf = _ab_post(proj, cs, p["gains"], p["w_uq"], t)
    kf, vf = _mla_kv(ckv, kr, p["w_ukv"], p["gk_n"])
    r3 = lambda a: a.reshape(b, t, a.shape[-1])
    if past is None:
        o_mla = _flash(r3(q), r3(kf), r3(vf), "chunk")
        o_sb = _sb_prompt(r3(sq), r3(skb), r3(svb))
    else:
        c_ckv, c_kr, c_sk, c_sv = past
        pl_ = c_ckv.shape[1]
        kc, vc = _mla_kv(c_ckv.reshape(b * pl_, KV_RANK), c_kr.reshape(b * pl_, ROPE_DIM), p["w_ukv"], p["gk_n"])
        o_mla = _step_softmax("mla", r3(q), kc.reshape(b, pl_, -1), vc.reshape(b, pl_, -1), r3(kf), r3(vf))
        o_sb = _step_sb(r3(sq), c_sk.reshape(b, pl_, HW), c_sv.reshape(b, pl_, HW), r3(skb), r3(svb))
    x = _proj_out(o_mla.reshape(n, HW), o_sb.reshape(n, HW), p["w_out_a"], p["w_out_b"], x)
    h4 = lambda a: a.reshape(b, t, N_HEADS, HEAD_DIM)
    return x, (ckv.reshape(b, t, KV_RANK), kr.reshape(b, t, ROPE_DIM), h4(skf), h4(svf))


def _tail_rows(a, n):
    t = a.shape[1]
    if t >= n:
        return a[:, t - n:]
    return jnp.pad(a, ((0, 0), (n - t, 0)) + ((0, 0),) * (a.ndim - 2))


def _cd_mixer(x, b, t, past, band_len, p):
    n = x.shape[0]
    proj = _proj_in(x, p["g_mix"], p["w_in"], 1280)
    bq, bkf, bkb, bvf, bvb, fq, fkf, fkb, fvf, fvb, lf = _cd_post(proj, p["gains"], p["f_bias"])
    r3 = lambda a: a.reshape(b, t, a.shape[-1])
    lf3 = r3(lf)
    if past is None:
        front = ((0, 0), (BAND_LEFT * CHUNK, 0), (0, 0))
        o_band = _band_prompt(r3(bq), jnp.pad(r3(bkb), front), jnp.pad(r3(bvb), front), p["rel_bias"])
        buf_k, buf_v = _tail_rows(r3(bkf), band_len), _tail_rows(r3(bvf), band_len)
        cum = _cumsum_rows(lf3)
        o_fox = _flash(r3(fq), r3(fkb), r3(fvb), "causal", jnp.swapaxes(cum, 1, 2), cum)
    else:
        c_bk, c_bv, c_fk, c_fv, c_lf = past
        pl_ = c_fk.shape[1]
        bl = c_bk.shape[1]
        o_band = _step_softmax("band", r3(bq), c_bk.reshape(b, bl, HW), c_bv.reshape(b, bl, HW), r3(bkb), r3(bvb),
                               p["rel_bias"][:, :, :bl], p["rel_bias"][:, :, bl:])
        buf_k = _tail_rows(jnp.concatenate([c_bk.reshape(b, bl, HW), r3(bkf)], 1), band_len)
        buf_v = _tail_rows(jnp.concatenate([c_bv.reshape(b, bl, HW), r3(bvf)], 1), band_len)
        cum = _cumsum_rows(jnp.concatenate([c_lf.astype(F32), lf3], 1))
        o_fox = _step_softmax("fox", r3(fq), c_fk.reshape(b, pl_, HW), c_fv.reshape(b, pl_, HW), r3(fkb), r3(fvb),
                              jnp.swapaxes(cum[:, :, pl_:], 1, 2), cum[:, :, :pl_], cum[:, :, pl_:])
    x = _proj_out(o_band.reshape(n, HW), o_fox.reshape(n, HW), p["w_out_a"], p["w_out_b"], x)
    h4 = lambda a: a.reshape(a.shape[0], a.shape[1], N_HEADS, HEAD_DIM)
    return x, (h4(buf_k), h4(buf_v), h4(r3(fkf)), h4(r3(fvf)), lf3)


def _trunk(x3, past_len, caches, band_len, layers):
    b, t, d = x3.shape
    x = x3.reshape(b * t, d)
    pos = jnp.arange(past_len, past_len + t)
    ab_states, cd_states = [], []
    for l, p in enumerate(layers):
        i = l // 2
        x = _ffn(x, p["g_ffn1"], *p["ffn1"])
        if l % 2 == 0:
            past = None if caches is None else tuple(c[i] for c in caches[:4])
            x, st = _ab_mixer(x, b, t, pos, past, p)
            ab_states.append(st)
        else:
            past = None if caches is None else tuple(c[i] for c in caches[4:])
            x, st = _cd_mixer(x, b, t, past, band_len, p)
            cd_states.append(st)
        x = _ffn(x, p["g_ffn2"], *p["ffn2"])
    states = [jnp.stack(s, 0) for s in zip(*ab_states)] + [jnp.stack(s, 0) for s in zip(*cd_states)]
    return x.reshape(b, t, d), states


def kernel(x_prompt, x_sample, cache_mla_ckv, cache_mla_krope, cache_sb_k, cache_sb_v, cache_band_k,
           cache_band_v, cache_fox_k, cache_fox_v, cache_fox_logf, norm_gain, ffn_w_gate, ffn_w_up,
           ffn_w_down, ab_w_in, mla_lat_gain, mla_w_uq, mla_w_ukv, mla_qk_gain, ab_w_out, cd_w_in,
           fox_f_bias, cd_qk_gain, band_rel_bias, cd_w_out):
    depth = norm_gain.shape[0]
    past_len = cache_fox_k.shape[2]
    band_len = cache_band_k.shape[2]
    t_step = x_sample.shape[1]
    assert t_step == CHUNK and past_len % CHUNK == 0 and band_len == BAND_LEFT * CHUNK
    assert x_prompt.shape[1] % CHUNK == 0

    layers = []
    for l in range(depth):
        i = l // 2
        if l % 2 == 0:
            p = _prep_ab(ab_w_in[i], mla_lat_gain[i], mla_w_uq[i], mla_w_ukv[i], mla_qk_gain[i], ab_w_out[i])
        else:
            p = _prep_cd(cd_w_in[i], fox_f_bias[i], cd_qk_gain[i], cd_w_out[i])
            p["rel_bias"] = _rel_bias(band_rel_bias[i])
        g = norm_gain[l]
        p["g_ffn1"], p["g_mix"], p["g_ffn2"] = g[0:1], g[1:2], g[2:3]
        p["ffn1"] = tuple(w[l, 0].astype(BF16) for w in (ffn_w_gate, ffn_w_up, ffn_w_down))
        p["ffn2"] = tuple(w[l, 1].astype(BF16) for w in (ffn_w_gate, ffn_w_up, ffn_w_down))
        layers.append(p)

    caches = (cache_mla_ckv, cache_mla_krope, cache_sb_k, cache_sb_v,
              cache_band_k, cache_band_v, cache_fox_k, cache_fox_v, cache_fox_logf)
    y_p, st_p = _trunk(x_prompt, 0, None, band_len, layers)
    y_s, st_s = _trunk(x_sample, past_len, caches, band_len, layers)
    out = [y_p, y_s]
    for a, c in zip(st_p, st_s):
        out += [a, c]
    return tuple(out)
```

```python
import functools

import jax
import jax.numpy as jnp
from jax import lax
from jax.experimental import pallas as pl
from jax.experimental.pallas import tpu as pltpu

F32 = jnp.float32
BF16 = jnp.bfloat16

EPS = 1e-6
NEG_INF = -1e30
CHUNK = 64
BAND_LEFT = 8
REL_CLIP = 128
N_HEADS = 8
HEAD_DIM = 128
Q_RANK = 512
KV_RANK = 256
ROPE_DIM = 64
NOPE_DIM = 128
QK_HEAD = NOPE_DIM + ROPE_DIM
MLA_QK_PAD = 256
ROPE_THETA = 10000.0
MLA_SCALE = QK_HEAD ** -0.5
ATT_SCALE = HEAD_DIM ** -0.5
LOG2E = 1.4426950408889634
HEAD_PAIR = 2
SB_HEADS = 2
STEP_SB_GROUP = 2
HW = N_HEADS * HEAD_DIM
AB_COLS = 4096
CD_COLS = 6400
BAND_KEYS = (BAND_LEFT + 1) * CHUNK

VMEM_LIMIT_BYTES = 48 * 1024 * 1024


def _params(*sem):
    return pltpu.CompilerParams(dimension_semantics=sem, vmem_limit_bytes=VMEM_LIMIT_BYTES)


def _rms(x, gain, inv_n=None):
    if inv_n is None:
        ms = jnp.mean(x * x, axis=-1, keepdims=True)
    else:
        ms = jnp.sum(x * x, axis=-1, keepdims=True) * inv_n
    return x * lax.rsqrt(ms + EPS) * gain


def _dot(a, b):
    return jnp.dot(a, b, preferred_element_type=F32)


def _dot_t(a, b):
    return lax.dot_general(a, b, (((1,), (1,)), ((), ())), preferred_element_type=F32)


def _ffn_kernel(x_ref, g_ref, wg_ref, wu_ref, wd_ref, o_ref, h_ref, *, n_f):
    j = pl.program_id(1)

    @pl.when(j == 0)
    def _():
        h_ref[...] = _rms(x_ref[...], g_ref[...]).astype(BF16)
        o_ref[...] = jnp.zeros_like(o_ref)

    h = h_ref[...]
    g = _dot(h, wg_ref[...])
    u = _dot(h, wu_ref[...])
    a = (g / (1.0 + jnp.exp(-g)) * u).astype(BF16)
    o_ref[...] += _dot(a, wd_ref[...])

    @pl.when(j == n_f - 1)
    def _():
        o_ref[...] = x_ref[...] + 0.5 * o_ref[...]


def _ffn(x, gain, wg, wu, wd, l, s):
    n, d = x.shape
    f = wg.shape[-1]
    tm = min(512, n)
    tf = 512 if f % 512 == 0 else f
    n_f = f // tf
    return pl.pallas_call(
        functools.partial(_ffn_kernel, n_f=n_f),
        grid=(n // tm, n_f),
        in_specs=[
            pl.BlockSpec((tm, d), lambda i, j: (i, 0)),
            pl.BlockSpec((1, d), lambda i, j: (0, 0)),
            pl.BlockSpec((None, None, d, tf), lambda i, j: (l, s, 0, j)),
            pl.BlockSpec((None, None, d, tf), lambda i, j: (l, s, 0, j)),
            pl.BlockSpec((None, None, tf, d), lambda i, j: (l, s, j, 0)),
        ],
        out_specs=pl.BlockSpec((tm, d), lambda i, j: (i, 0)),
        out_shape=jax.ShapeDtypeStruct((n, d), F32),
        scratch_shapes=[pltpu.VMEM((tm, d), BF16)],
        compiler_params=_params("parallel", "arbitrary"),
        name="ffn",
    )(x, gain, wg, wu, wd)


def _cast_kernel(x_ref, o_ref):
    o_ref[...] = x_ref[...].astype(BF16)


def _to_bf16(w):
    shape = w.shape
    w2 = w.reshape(-1, shape[-1])
    r, c = w2.shape
    tr = 256 if r % 256 == 0 else r
    out = pl.pallas_call(
        _cast_kernel,
        grid=(r // tr,),
        in_specs=[pl.BlockSpec((tr, c), lambda i: (i, 0))],
        out_specs=pl.BlockSpec((tr, c), lambda i: (i, 0)),
        out_shape=jax.ShapeDtypeStruct((r, c), BF16),
        compiler_params=_params("parallel"),
        name="to_bf16",
    )(w2)
    return out.reshape(shape)


def _proj_in_kernel(x_ref, g_ref, w_ref, o_ref, h_ref):
    @pl.when(pl.program_id(1) == 0)
    def _():
        h_ref[...] = _rms(x_ref[...], g_ref[...]).astype(BF16)

    o_ref[...] = _dot(h_ref[...], w_ref[...])


def _proj_in(x, gain, w, tn):
    n, d = x.shape
    c = w.shape[1]
    tm = min(512, n)
    return pl.pallas_call(
        _proj_in_kernel,
        grid=(n // tm, c // tn),
        in_specs=[
            pl.BlockSpec((tm, d), lambda i, j: (i, 0)),
            pl.BlockSpec((1, d), lambda i, j: (0, 0)),
            pl.BlockSpec((d, tn), lambda i, j: (0, j)),
        ],
        out_specs=pl.BlockSpec((tm, tn), lambda i, j: (i, j)),
        out_shape=jax.ShapeDtypeStruct((n, c), F32),
        scratch_shapes=[pltpu.VMEM((tm, d), BF16)],
        compiler_params=_params("parallel", "arbitrary"),
        name="proj_in",
    )(x, gain, w)


def _proj_out_kernel(a_ref, b_ref, wa_ref, wb_ref, x_ref, o_ref):
    o_ref[...] = x_ref[...] + _dot(a_ref[...], wa_ref[...]) + _dot(b_ref[...], wb_ref[...])


def _proj_out(oa, ob, wa, wb, x):
    n, d = x.shape
    k = oa.shape[1]
    tm = min(512, n)
    tn = min(1024, d)
    return pl.pallas_call(
        _proj_out_kernel,
        grid=(n // tm, d // tn),
        in_specs=[
            pl.BlockSpec((tm, k), lambda i, j: (i, 0)),
            pl.BlockSpec((tm, k), lambda i, j: (i, 0)),
            pl.BlockSpec((k, tn), lambda i, j: (0, j)),
            pl.BlockSpec((k, tn), lambda i, j: (0, j)),
            pl.BlockSpec((tm, tn), lambda i, j: (i, j)),
        ],
        out_specs=pl.BlockSpec((tm, tn), lambda i, j: (i, j)),
        out_shape=jax.ShapeDtypeStruct((n, d), F32),
        compiler_params=_params("parallel", "parallel"),
        name="proj_out",
    )(oa, ob, wa, wb, x)


def _rope_slab(x, cs):
    c = cs[:, 0:128]
    s1 = cs[:, 128:256]
    s2 = cs[:, 256:384]
    return x * c + pltpu.roll(x, 96, 1) * s1 + pltpu.roll(x, 32, 1) * s2


def _ab_post_kernel(sq_ref, sk_ref, sv_ref, tail_ref, cs_ref, gn_ref, wuq_ref,
                    q_ref, ckv_ref, kr_ref, sqb_ref, skb_ref, svb_ref, skf_ref, svf_ref):
    sqb_ref[...] = (sq_ref[...] * ATT_SCALE).astype(BF16)
    sk = sk_ref[...]
    sv = sv_ref[...]
    skf_ref[...] = sk
    svf_ref[...] = sv
    skb_ref[...] = sk.astype(BF16)
    svb_ref[...] = sv.astype(BF16)

    cs = cs_ref[...]
    g_cq = gn_ref[0:1, :]
    g_ckv = gn_ref[1:2, 0:KV_RANK]
    gq_n = gn_ref[2:3, 0:128]
    gq_r = gn_ref[3:4, 0:128]
    gk_r = gn_ref[4:5, 0:128]

    cq = _rms(tail_ref[:, 0:Q_RANK], g_cq)
    ckv_ref[...] = _rms(tail_ref[:, Q_RANK:Q_RANK + KV_RANK], g_ckv)
    kr = _rms(tail_ref[:, 768:896], gk_r, 1.0 / ROPE_DIM)
    kr_ref[...] = _rope_slab(kr, cs)[:, 0:ROPE_DIM]

    q = _dot(cq.astype(BF16), wuq_ref[...])
    for h in range(N_HEADS):
        lo = h * MLA_QK_PAD
        qn = _rms(q[:, lo:lo + 128], gq_n)
        qr = _rope_slab(_rms(q[:, lo + 128:lo + 256], gq_r, 1.0 / ROPE_DIM), cs)
        q_ref[:, lo:lo + 128] = (qn * (MLA_SCALE * LOG2E)).astype(BF16)
        q_ref[:, lo + 128:lo + 256] = (qr * (MLA_SCALE * LOG2E)).astype(BF16)


def _ab_post(proj, cs_tab, gains, wuq):
    n = proj.shape[0]
    tm = min(256, n, cs_tab.shape[0])
    n_tab = cs_tab.shape[0] // tm
    row = lambda i: (i, 0)
    return pl.pallas_call(
        _ab_post_kernel,
        grid=(n // tm,),
        in_specs=[
            pl.BlockSpec((tm, HW), lambda i: (i, 0)),
            pl.BlockSpec((tm, HW), lambda i: (i, 1)),
            pl.BlockSpec((tm, HW), lambda i: (i, 2)),
            pl.BlockSpec((tm, HW), lambda i: (i, 3)),
            pl.BlockSpec((tm, 384), lambda i: (i % n_tab, 0)),
            pl.BlockSpec(gains.shape, lambda i: (0, 0)),
            pl.BlockSpec(wuq.shape, lambda i: (0, 0)),
        ],
        out_specs=[
            pl.BlockSpec((tm, N_HEADS * MLA_QK_PAD), row),
            pl.BlockSpec((tm, KV_RANK), row),
            pl.BlockSpec((tm, ROPE_DIM), row),
            pl.BlockSpec((tm, HW), row),
            pl.BlockSpec((tm, HW), row),
            pl.BlockSpec((tm, HW), row),
            pl.BlockSpec((tm, HW), row),
            pl.BlockSpec((tm, HW), row),
        ],
        out_shape=[
            jax.ShapeDtypeStruct((n, N_HEADS * MLA_QK_PAD), BF16),
            jax.ShapeDtypeStruct((n, KV_RANK), F32),
            jax.ShapeDtypeStruct((n, ROPE_DIM), F32),
            jax.ShapeDtypeStruct((n, HW), BF16),
            jax.ShapeDtypeStruct((n, HW), BF16),
            jax.ShapeDtypeStruct((n, HW), BF16),
            jax.ShapeDtypeStruct((n, HW), F32),
            jax.ShapeDtypeStruct((n, HW), F32),
        ],
        compiler_params=_params("parallel"),
        name="ab_post",
    )(proj, proj, proj, proj, cs_tab, gains, wuq)


def _mla_kv_kernel(ckv_ref, kr_ref, w_ref, g_ref, k_ref, v_ref):
    kv = _dot(ckv_ref[...].astype(BF16), w_ref[...])
    m = kv.shape[0]
    kr = jnp.concatenate([kr_ref[...], jnp.zeros((m, 128 - ROPE_DIM), F32)], axis=1).astype(BF16)
    g = g_ref[...]
    for h in range(N_HEADS):
        lo = h * 256
        k_ref[:, lo:lo + 128] = _rms(kv[:, lo:lo + 128], g).astype(BF16)
        k_ref[:, lo + 128:lo + 256] = kr
        v_ref[:, h * 128:(h + 1) * 128] = kv[:, lo + 128:lo + 256].astype(BF16)


def _mla_kv(ckv, kr, w_ukv, gk_n):
    m = ckv.shape[0]
    tm = min(512, m)
    row = lambda i: (i, 0)
    return pl.pallas_call(
        _mla_kv_kernel,
        grid=(m // tm,),
        in_specs=[
            pl.BlockSpec((tm, KV_RANK), row),
            pl.BlockSpec((tm, ROPE_DIM), row),
            pl.BlockSpec(w_ukv.shape, lambda i: (0, 0)),
            pl.BlockSpec((1, 128), lambda i: (0, 0)),
        ],
        out_specs=[
            pl.BlockSpec((tm, N_HEADS * MLA_QK_PAD), row),
            pl.BlockSpec((tm, HW), row),
        ],
        out_shape=[
            jax.ShapeDtypeStruct((m, N_HEADS * MLA_QK_PAD), BF16),
            jax.ShapeDtypeStruct((m, HW), BF16),
        ],
        compiler_params=_params("parallel"),
        name="mla_kv",
    )(ckv, kr, w_ukv, gk_n)


def _cd_post_kernel(bq_ref, bk_ref, bv_ref, fq_ref, fk_ref, fv_ref, fl_ref, gn_ref, fb_ref,
                    bqb_ref, bkf_ref, bkb_ref, bvf_ref, bvb_ref,
                    fqb_ref, fkf_ref, fkb_ref, fvf_ref, fvb_ref, lf_ref, *, pad_tiles):
    live = pl.program_id(1) >= pad_tiles
    bv = bv_ref[...]
    bvf_ref[...] = bv
    bvb_ref[...] = jnp.where(live, bv, 0.0).astype(BF16)
    fv = fv_ref[...]
    fvf_ref[...] = fv
    fvb_ref[...] = fv.astype(BF16)
    for h in range(N_HEADS):
        sl = slice(h * 128, (h + 1) * 128)
        bqb_ref[:, sl] = (_rms(bq_ref[:, sl], gn_ref[0:1, :]) * (ATT_SCALE * LOG2E)).astype(BF16)
        bk = _rms(bk_ref[:, sl], gn_ref[1:2, :])
        bkf_ref[:, sl] = bk
        bkb_ref[:, sl] = jnp.where(live, bk, 0.0).astype(BF16)
        fqb_ref[:, sl] = (_rms(fq_ref[:, sl], gn_ref[2:3, :]) * (ATT_SCALE * LOG2E)).astype(BF16)
        fk = _rms(fk_ref[:, sl], gn_ref[3:4, :])
        fkf_ref[:, sl] = fk
        fkb_ref[:, sl] = fk.astype(BF16)
    z = fl_ref[:, 0:128] + fb_ref[...]
    lf = -(jnp.maximum(-z, 0.0) + jnp.log1p(jnp.exp(-jnp.abs(z))))
    lf_ref[...] = lf[:, 0:N_HEADS]


def _cd_post(proj, gains, f_bias, b, t, pad_rows):
    n = proj.shape[0]
    tm = min(256, t)
    tpb = t // tm
    pad_tiles = pad_rows // tm
    src = lambda bi, s: bi * tpb + jnp.maximum(s - pad_tiles, 0)
    col = lambda c: pl.BlockSpec((tm, HW), lambda bi, s, c=c: (src(bi, s), c))
    f32o = jax.ShapeDtypeStruct((n, HW), F32)
    b16o = jax.ShapeDtypeStruct((n, HW), BF16)
    padded = jax.ShapeDtypeStruct((b * (pad_rows + t), HW), BF16)
    blk = pl.BlockSpec((tm, HW), lambda bi, s: (src(bi, s), 0))
    pblk = pl.BlockSpec((tm, HW), lambda bi, s: (bi * (tpb + pad_tiles) + s, 0))
    return pl.pallas_call(
        functools.partial(_cd_post_kernel, pad_tiles=pad_tiles),
        grid=(b, tpb + pad_tiles),
        in_specs=[col(0), col(1), col(2), col(3), col(4), col(5),
                  pl.BlockSpec((tm, 256), lambda bi, s: (src(bi, s), 6 * HW // 256)),
                  pl.BlockSpec((4, 128), lambda bi, s: (0, 0)),
                  pl.BlockSpec((1, 128), lambda bi, s: (0, 0))],
        out_specs=[blk, blk, pblk, blk, pblk, blk, blk, blk, blk, blk,
                   pl.BlockSpec((tm, N_HEADS), lambda bi, s: (src(bi, s), 0))],
        out_shape=[b16o, f32o, padded, f32o, padded, b16o, f32o, b16o, f32o, b16o,
                   jax.ShapeDtypeStruct((n, N_HEADS), F32)],
        compiler_params=_params("arbitrary", "arbitrary"),
        name="cd_post",
    )(proj, proj, proj, proj, proj, proj, proj, gains, f_bias)


def _cumsum_kernel(x_ref, o_ref, *, n_blk, scale):
    jj = lax.broadcasted_iota(jnp.int32, (128, 128), 0)
    ss = lax.broadcasted_iota(jnp.int32, (128, 128), 1)
    upper = jnp.where(jj <= ss, 1.0, 0.0).astype(BF16)

    def body(i, carry):
        x = x_ref[i]
        hi = x.astype(BF16)
        r1 = x - hi.astype(F32)
        mid = r1.astype(BF16)
        lo = (r1 - mid.astype(F32)).astype(BF16)
        c = _dot(hi, upper) + _dot(mid, upper) + _dot(lo, upper) + carry
        o_ref[i] = c * scale
        return c[:, 127:128]

    lax.fori_loop(0, n_blk, body, jnp.zeros((N_HEADS, 1), F32))


def _cumsum_rows(lf, scale):
    b, length, nh = lf.shape
    n_blk = -(-length // 128)
    x = jnp.pad(jnp.swapaxes(lf, 1, 2), ((0, 0), (0, 0), (0, n_blk * 128 - length)))
    x = jnp.swapaxes(x.reshape(b, nh, n_blk, 128), 1, 2)
    out = pl.pallas_call(
        functools.partial(_cumsum_kernel, n_blk=n_blk, scale=scale),
        grid=(b,),
        in_specs=[pl.BlockSpec((None, n_blk, nh, 128), lambda i: (i, 0, 0, 0))],
        out_specs=pl.BlockSpec((None, n_blk, nh, 128), lambda i: (i, 0, 0, 0)),
        out_shape=jax.ShapeDtypeStruct((b, n_blk, nh, 128), F32),
        compiler_params=_params("parallel"),
        name="cumsum",
    )(x)
    return jnp.swapaxes(out, 1, 2).reshape(b, nh, n_blk * 128)[:, :, :length]


def _rel_bias_kernel(tab_ref, o_ref, *, n_tab):
    h = pl.program_id(0)
    i = lax.broadcasted_iota(jnp.int32, (CHUNK, BAND_KEYS), 0)
    j = lax.broadcasted_iota(jnp.int32, (CHUNK, BAND_KEYS), 1)
    idx = jnp.clip(BAND_LEFT * CHUNK + i - j, -REL_CLIP, REL_CLIP) + REL_CLIP

    def body(r, acc):
        return jnp.where(idx == r, tab_ref[h, r] * LOG2E, acc)

    o_ref[...] = lax.fori_loop(0, n_tab, body, jnp.zeros((CHUNK, BAND_KEYS), F32))


def _rel_bias(table):
    nh, n_tab = table.shape
    return pl.pallas_call(
        functools.partial(_rel_bias_kernel, n_tab=n_tab),
        grid=(nh,),
        in_specs=[pl.BlockSpec(memory_space=pltpu.SMEM)],
        out_specs=pl.BlockSpec((None, CHUNK, BAND_KEYS), lambda h: (h, 0, 0)),
        out_shape=jax.ShapeDtypeStruct((nh, CHUNK, BAND_KEYS), F32),
        compiler_params=_params("arbitrary"),
        name="rel_bias",
    )(table)


def _flash_kernel(*refs, tq, mode, has_bias):
    if has_bias:
        q_ref, k_ref, v_ref, cq_ref, ck_ref, o_ref, m_scr, acc_scr = refs
    else:
        q_ref, k_ref, v_ref, o_ref, m_scr, acc_scr = refs
    hg = pl.program_id(1)
    qi = pl.program_id(2)
    dk = q_ref.shape[1] // HEAD_PAIR
    reps = tq // HEAD_DIM
    m_scr[...] = jnp.full(m_scr.shape, NEG_INF, F32)
    acc_scr[...] = jnp.zeros_like(acc_scr)
    ones = jnp.ones((tq, HEAD_DIM), BF16)
    if has_bias:
        lane = lax.broadcasted_iota(jnp.int32, cq_ref.shape, 1)
        cqs = [jnp.broadcast_to(
            jnp.sum(jnp.where(lane == hg * HEAD_PAIR + j, cq_ref[...], 0.0), axis=-1, keepdims=True),
            (tq, HEAD_DIM)) for j in range(HEAD_PAIR)]

    def step(kb, masked):
        ks = pl.multiple_of(kb * tq, tq)
        scores = [_dot_t(q_ref[:, j * dk:(j + 1) * dk], k_ref[pl.ds(ks, tq), j * dk:(j + 1) * dk])
                  for j in range(HEAD_PAIR)]
        for j in range(HEAD_PAIR):
            s = scores[j]
            if has_bias:
                s = s - ck_ref[j, pl.ds(kb, 1), :]
            if masked:
                row = lax.broadcasted_iota(jnp.int32, (tq, tq), 0)
                col = lax.broadcasted_iota(jnp.int32, (tq, tq), 1)
                if mode == "chunk":
                    ok = lax.shift_right_logical(col, 6) <= lax.shift_right_logical(row, 6)
                else:
                    ok = col <= row
                s = jnp.where(ok, s, NEG_INF)
            m_old = m_scr[j]
            rmax = jnp.max(s, axis=-1, keepdims=True)
            if has_bias:
                m_new = jnp.maximum(m_old, rmax + cqs[j])
                shift = m_new - cqs[j]
            else:
                m_new = jnp.maximum(m_old, rmax)
                shift = m_new
            p = jnp.exp2(s - pltpu.repeat(shift, reps, 1))
            alpha = jnp.exp2(m_old - m_new)
            v1 = jnp.concatenate([v_ref[pl.ds(ks, tq), j * HEAD_DIM:(j + 1) * HEAD_DIM], ones], axis=1)
            acc_scr[j] = pltpu.repeat(alpha, 2, 1) * acc_scr[j] + _dot(p.astype(BF16), v1)
            m_scr[j] = m_new

    def body(kb, c):
        step(kb, False)
        return c

    lax.fori_loop(0, qi, body, 0)
    step(qi, True)
    for j in range(HEAD_PAIR):
        acc = acc_scr[j]
        o_ref[:, j * HEAD_DIM:(j + 1) * HEAD_DIM] = (acc[:, :HEAD_DIM] / acc[:, HEAD_DIM:]).astype(BF16)


def _flash(q, k, v, mode, cq=None, ck=None):
    b, t, _ = q.shape
    dk = q.shape[2] // N_HEADS
    tq = min(512, t)
    nq = t // tq
    has_bias = cq is not None
    hp = HEAD_PAIR
    in_specs = [
        pl.BlockSpec((None, tq, hp * dk), lambda bi, h, qi: (bi, qi, h)),
        pl.BlockSpec((None, t, hp * dk), lambda bi, h, qi: (bi, 0, h)),
        pl.BlockSpec((None, t, hp * HEAD_DIM), lambda bi, h, qi: (bi, 0, h)),
    ]
    args = [q, k, v]
    if has_bias:
        in_specs += [
            pl.BlockSpec((None, tq, N_HEADS), lambda bi, h, qi: (bi, qi, 0)),
            pl.BlockSpec((None, hp, nq, tq), lambda bi, h, qi: (bi, h, 0, 0)),
        ]
        args += [cq, ck.reshape(b, N_HEADS, nq, tq)]
    return pl.pallas_call(
        functools.partial(_flash_kernel, tq=tq, mode=mode, has_bias=has_bias),
        grid=(b, N_HEADS // hp, nq),
        in_specs=in_specs,
        out_specs=pl.BlockSpec((None, tq, hp * HEAD_DIM), lambda bi, h, qi: (bi, qi, h)),
        out_shape=jax.ShapeDtypeStruct((b, t, HW), BF16),
        scratch_shapes=[pltpu.VMEM((hp, tq, HEAD_DIM), F32), pltpu.VMEM((hp, tq, 2 * HEAD_DIM), F32)],
        compiler_params=_params("parallel", "parallel", "arbitrary"),
        name="flash_" + mode,
    )(*args)


def _sb_stage1(z, tri, ok):
    n = z.shape[1]
    sp = jnp.maximum(z, 0.0) + jnp.log(1.0 + jnp.exp(-jnp.abs(z)))
    lsig = z - sp
    if ok is not None:
        sp = jnp.where(ok, sp, 0.0)
    hi = sp.astype(BF16)
    lo = (sp - hi.astype(F32)).astype(BF16)
    if n % HEAD_DIM == 0:
        right = _dot(jnp.concatenate([hi, lo], axis=1), tri)
    else:
        right = _dot(hi, tri[:n]) + _dot(lo, tri[:n])
    return lsig, right, jnp.sum(sp, axis=-1, keepdims=True)


def _sb_stage2(lsig, right, c, v, ok):
    n = lsig.shape[1]
    c_wide = pltpu.repeat(c, n // HEAD_DIM, 1) if n % HEAD_DIM == 0 else c[:, :n]
    w = jnp.exp(lsig - right + c_wide)
    if ok is not None:
        w = jnp.where(ok, w, 0.0)
    return _dot(w.astype(BF16), v)


def _tri(n):
    jj = lax.broadcasted_iota(jnp.int32, (2 * n, n), 0)
    ss = lax.broadcasted_iota(jnp.int32, (2 * n, n), 1)
    return jnp.where(jnp.where(jj >= n, jj - n, jj) > ss, 1.0, 0.0).astype(BF16)


def _sb_kernel(q_ref, k_ref, v_ref, o_ref, c_scr, acc_scr, *, tq, sub):
    qi = pl.program_id(2)
    n_sub = tq // sub
    tri = _tri(sub)
    c_scr[...] = jnp.zeros_like(c_scr)
    acc_scr[...] = jnp.zeros_like(acc_scr)

    def block(kb, diag):
        chains = [(d, j) for d in reversed(range(n_sub)) for j in range(SB_HEADS)]
        hs = lambda j: slice(j * HEAD_DIM, (j + 1) * HEAD_DIM)
        ks = lambda d: pl.multiple_of(kb * tq + d * sub, sub)
        oks = {}
        for d in range(n_sub):
            oks[d] = None
            if diag:
                row = lax.broadcasted_iota(jnp.int32, (tq, sub), 0)
                col = lax.broadcasted_iota(jnp.int32, (tq, sub), 1)
                oks[d] = col + d * sub < row
        zs = {(d, j): _dot_t(q_ref[:, hs(j)], k_ref[pl.ds(ks(d), sub), hs(j)]) for d, j in chains}
        mids = {dj: _sb_stage1(zs[dj], tri, oks[dj[0]]) for dj in chains}
        for d, j in chains:
            lsig, right, rowsum = mids[(d, j)]
            c = c_scr[j]
            acc_scr[j] += _sb_stage2(lsig, right, c, v_ref[pl.ds(ks(d), sub), hs(j)], oks[d])
            c_scr[j] = c - rowsum

    block(qi, True)

    def body(i, c):
        block(qi - 1 - i, False)
        return c

    lax.fori_loop(0, qi, body, 0)
    for j in range(SB_HEADS):
        o_ref[:, j * HEAD_DIM:(j + 1) * HEAD_DIM] = acc_scr[j].astype(BF16)


def _sb_prompt(q, k, v):
    b, t, _ = q.shape
    tq = min(512, t)
    sub = min(256, tq)
    hp = SB_HEADS
    return pl.pallas_call(
        functools.partial(_sb_kernel, tq=tq, sub=sub),
        grid=(b, N_HEADS // hp, t // tq),
        in_specs=[
            pl.BlockSpec((None, tq, hp * HEAD_DIM), lambda bi, h, qi: (bi, qi, h)),
            pl.BlockSpec((None, t, hp * HEAD_DIM), lambda bi, h, qi: (bi, 0, h)),
            pl.BlockSpec((None, t, hp * HEAD_DIM), lambda bi, h, qi: (bi, 0, h)),
        ],
        out_specs=pl.BlockSpec((None, tq, hp * HEAD_DIM), lambda bi, h, qi: (bi, qi, h)),
        out_shape=jax.ShapeDtypeStruct((b, t, HW), BF16),
        scratch_shapes=[pltpu.VMEM((hp, tq, HEAD_DIM), F32), pltpu.VMEM((hp, tq, HEAD_DIM), F32)],
        compiler_params=_params("parallel", "parallel", "arbitrary"),
        name="sb_prompt",
    )(q, k, v)


def _band_kernel(q_ref, k_ref, v_ref, b_ref, o_ref, *, n_chunks, unroll):
    col = lax.broadcasted_iota(jnp.int32, (CHUNK, BAND_KEYS), 1)
    ones = jnp.ones((BAND_KEYS, HEAD_DIM), BF16)

    def body(i, carry):
        chains = [(u, j) for u in range(unroll) for j in range(HEAD_PAIR)]
        hs = lambda j: slice(j * HEAD_DIM, (j + 1) * HEAD_DIM)
        qs = lambda u: pl.multiple_of((i * unroll + u) * CHUNK, CHUNK)
        scores = {(u, j): _dot_t(q_ref[pl.ds(qs(u), CHUNK), hs(j)], k_ref[pl.ds(qs(u), BAND_KEYS), hs(j)])
                  for u, j in chains}
        for u, j in chains:
            ok = col >= (BAND_LEFT - (i * unroll + u)) * CHUNK
            s = jnp.where(ok, scores[(u, j)] + b_ref[j], NEG_INF)
            p = jnp.exp2(s - jnp.max(s, axis=-1, keepdims=True))
            r = _dot(p.astype(BF16), jnp.concatenate([v_ref[pl.ds(qs(u), BAND_KEYS), hs(j)], ones], axis=1))
            o_ref[pl.ds(qs(u), CHUNK), hs(j)] = (r[:, :HEAD_DIM] / r[:, HEAD_DIM:]).astype(BF16)
        return carry

    lax.fori_loop(0, n_chunks // unroll, body, 0)


def _band_prompt(q, k, v, bias):
    b, t, _ = q.shape
    tp = k.shape[1]
    hp = HEAD_PAIR
    n_chunks = t // CHUNK
    unroll = 4 if n_chunks % 4 == 0 else 1
    return pl.pallas_call(
        functools.partial(_band_kernel, n_chunks=n_chunks, unroll=unroll),
        grid=(b, N_HEADS // hp),
        in_specs=[
            pl.BlockSpec((None, t, hp * HEAD_DIM), lambda bi, h: (bi, 0, h)),
            pl.BlockSpec((None, tp, hp * HEAD_DIM), lambda bi, h: (bi, 0, h)),
            pl.BlockSpec((None, tp, hp * HEAD_DIM), lambda bi, h: (bi, 0, h)),
            pl.BlockSpec((hp, CHUNK, BAND_KEYS), lambda bi, h: (h, 0, 0)),
        ],
        out_specs=pl.BlockSpec((None, t, hp * HEAD_DIM), lambda bi, h: (bi, 0, h)),
        out_shape=jax.ShapeDtypeStruct((b, t, HW), BF16),
        compiler_params=_params("parallel", "parallel"),
        name="band_prompt",
    )(q, k, v, bias)


def _step_softmax_kernel(*refs, mode):
    if mode == "fox":
        q_ref, kp_ref, vp_ref, kn_ref, vn_ref, cq_ref, ckp_ref, ckn_ref, o_ref = refs
    elif mode == "band":
        q_ref, kp_ref, vp_ref, kn_ref, vn_ref, bp_ref, bn_ref, o_ref = refs
    else:
        q_ref, kp_ref, vp_ref, kn_ref, vn_ref, o_ref = refs
    t = q_ref.shape[0]
    dk = q_ref.shape[1] // N_HEADS
    scores = []
    for h in range(N_HEADS):
        ks = slice(h * dk, (h + 1) * dk)
        scores.append((_dot_t(q_ref[:, ks], kp_ref[:, ks].astype(BF16)), _dot_t(q_ref[:, ks], kn_ref[:, ks])))
    for h in range(N_HEADS):
        vs = slice(h * HEAD_DIM, (h + 1) * HEAD_DIM)
        sp, sn = scores[h]
        if mode == "fox":
            cq = cq_ref[:, h:h + 1]
            sp = sp + (cq - ckp_ref[h:h + 1, :])
            sn = sn + (cq - ckn_ref[h:h + 1, :])
            row = lax.broadcasted_iota(jnp.int32, (t, t), 0)
            col = lax.broadcasted_iota(jnp.int32, (t, t), 1)
            sn = jnp.where(col <= row, sn, NEG_INF)
        elif mode == "band":
            sp = sp + bp_ref[h]
            sn = sn + bn_ref[h]
        m = jnp.maximum(jnp.max(sp, axis=-1, keepdims=True), jnp.max(sn, axis=-1, keepdims=True))
        pp = jnp.exp2(sp - m)
        pn = jnp.exp2(sn - m)
        l = jnp.sum(pp, axis=-1, keepdims=True) + jnp.sum(pn, axis=-1, keepdims=True)
        o = _dot(pp.astype(BF16), vp_ref[:, vs].astype(BF16)) + _dot(pn.astype(BF16), vn_ref[:, vs])
        o_ref[:, vs] = (o / l).astype(BF16)


def _step_softmax(mode, q, kp, vp, kn, vn, *extra):
    b, t, qw = q.shape
    p = kp.shape[1]
    full = lambda a: pl.BlockSpec((None,) + a.shape[1:], lambda i: (i,) + (0,) * (a.ndim - 1))
    in_specs = [full(q), full(kp), full(vp), full(kn), full(vn)]
    if mode == "fox":
        in_specs += [full(e) for e in extra]
    elif mode == "band":
        in_specs += [pl.BlockSpec(e.shape, lambda i: (0, 0, 0)) for e in extra]
    return pl.pallas_call(
        functools.partial(_step_softmax_kernel, mode=mode),
        grid=(b,),
        in_specs=in_specs,
        out_specs=pl.BlockSpec((None, t, HW), lambda i: (i, 0, 0)),
        out_shape=jax.ShapeDtypeStruct((b, t, HW), BF16),
        compiler_params=_params("parallel"),
        name="step_" + mode,
    )(q, kp, vp, kn, vn, *extra)


def _step_sb_kernel(q_ref, kp_ref, vp_ref, kn_ref, vn_ref, o_ref, *, blk):
    t = q_ref.shape[0]
    p = kp_ref.shape[0]
    tri_n = _tri(t)
    tri_p = _tri(blk)
    row = lax.broadcasted_iota(jnp.int32, (t, t), 0)
    col = lax.broadcasted_iota(jnp.int32, (t, t), 1)
    blocks = list(reversed(range(p // blk)))
    hsl = lambda h: slice(h * HEAD_DIM, (h + 1) * HEAD_DIM)
    rsl = lambda kb: slice(kb * blk, (kb + 1) * blk)
    for h0 in range(0, N_HEADS, STEP_SB_GROUP):
        heads = range(h0, h0 + STEP_SB_GROUP)
        mids = {}
        for h in heads:
            mids[(h, None)] = _sb_stage1(_dot_t(q_ref[:, hsl(h)], kn_ref[:, hsl(h)]), tri_n, col < row)
            for kb in blocks:
                mids[(h, kb)] = _sb_stage1(_dot_t(q_ref[:, hsl(h)], kp_ref[rsl(kb), hsl(h)].astype(BF16)), tri_p, None)
        for h in heads:
            lsig, right, rowsum = mids[(h, None)]
            c = jnp.zeros((t, HEAD_DIM), F32)
            acc = _sb_stage2(lsig, right, c, vn_ref[:, hsl(h)], col < row)
            c = c - rowsum
            for kb in blocks:
                lsig, right, rowsum = mids[(h, kb)]
                acc = acc + _sb_stage2(lsig, right, c, vp_ref[rsl(kb), hsl(h)].astype(BF16), None)
                c = c - rowsum
            o_ref[:, hsl(h)] = acc.astype(BF16)


def _step_sb(q, kp, vp, kn, vn):
    b, t, _ = q.shape
    p = kp.shape[1]
    blk = min(256, p)
    full = lambda a: pl.BlockSpec((None,) + a.shape[1:], lambda i: (i, 0, 0))
    return pl.pallas_call(
        functools.partial(_step_sb_kernel, blk=blk),
        grid=(b,),
        in_specs=[full(q), full(kp), full(vp), full(kn), full(vn)],
        out_specs=pl.BlockSpec((None, t, HW), lambda i: (i, 0, 0)),
        out_shape=jax.ShapeDtypeStruct((b, t, HW), BF16),
        compiler_params=_params("parallel"),
        name="step_sb",
    )(q, kp, vp, kn, vn)


def _rope_tables(pos, rows):
    half = ROPE_DIM // 2
    inv = ROPE_THETA ** (-jnp.arange(half, dtype=F32) / half)
    ang = pos.astype(F32)[:, None] * inv[None, :]
    c, s, z = jnp.cos(ang), jnp.sin(ang), jnp.zeros_like(ang)
    tab = jnp.concatenate([c, c, z, z, -s, z, z, z, z, s, z, z], axis=1)
    return jnp.tile(tab, (max(1, rows // tab.shape[0]), 1))


def _prep_ab(w_in, lat_gain, w_uq, w_ukv, qk_gain, w_out):
    d = w_in.shape[0]
    cq, ckv, kr, sq, sk, sv = jnp.split(w_in, [512, 768, 832, 1856, 2880], axis=1)
    w_in_p = jnp.concatenate([sq, sk, sv, cq, ckv, kr, jnp.zeros((d, AB_COLS - w_in.shape[1]), F32)], axis=1)
    uq = w_uq.reshape(Q_RANK, N_HEADS, QK_HEAD)
    uq = jnp.pad(uq, ((0, 0), (0, 0), (0, MLA_QK_PAD - QK_HEAD))).reshape(Q_RANK, N_HEADS * MLA_QK_PAD)
    gains = jnp.zeros((8, Q_RANK), F32)
    gains = gains.at[0, :].set(lat_gain[:Q_RANK])
    gains = gains.at[1, :KV_RANK].set(lat_gain[Q_RANK:])
    gains = gains.at[2, :NOPE_DIM].set(qk_gain[0, :NOPE_DIM])
    gains = gains.at[3, :ROPE_DIM].set(qk_gain[0, NOPE_DIM:])
    gains = gains.at[4, :ROPE_DIM].set(qk_gain[1, NOPE_DIM:])
    return dict(w_in=w_in_p.astype(BF16), w_uq=uq.astype(BF16), w_ukv=w_ukv.astype(BF16), gains=gains,
                gk_n=qk_gain[1:2, :NOPE_DIM], w_out_a=w_out[:HW].astype(BF16), w_out_b=w_out[HW:].astype(BF16))


def _prep_cd(w_in, f_bias, qk_gain, w_out):
    d = w_in.shape[0]
    w_in_p = jnp.pad(w_in, ((0, 0), (0, CD_COLS - w_in.shape[1])))
    fb = jnp.pad(f_bias, (0, 128 - N_HEADS)).reshape(1, 128)
    return dict(w_in=w_in_p.astype(BF16), gains=qk_gain, f_bias=fb,
                w_out_a=w_out[:HW].astype(BF16), w_out_b=w_out[HW:].astype(BF16))


def _ab_mixer(x, b, t, pos, past, p):
    n = x.shape[0]
    proj = _proj_in(x, p["g_mix"], p["w_in"], 1024)
    cs = _rope_tables(pos, min(256, n))
    q, ckv, kr, sq, skb, svb, skf, svf = _ab_post(proj, cs, p["gains"], p["w_uq"])
    kf, vf = _mla_kv(ckv, kr, p["w_ukv"], p["gk_n"])
    r3 = lambda a: a.reshape(b, t, a.shape[-1])
    if past is None:
        o_mla = _flash(r3(q), r3(kf), r3(vf), "chunk")
        o_sb = _sb_prompt(r3(sq), r3(skb), r3(svb))
    else:
        c_ckv, c_kr, c_sk, c_sv = past
        pl_ = c_ckv.shape[1]
        kc, vc = _mla_kv(c_ckv.reshape(b * pl_, KV_RANK), c_kr.reshape(b * pl_, ROPE_DIM), p["w_ukv"], p["gk_n"])
        o_mla = _step_softmax("mla", r3(q), kc.reshape(b, pl_, -1), vc.reshape(b, pl_, -1), r3(kf), r3(vf))
        o_sb = _step_sb(r3(sq), c_sk.reshape(b, pl_, HW), c_sv.reshape(b, pl_, HW), r3(skb), r3(svb))
    x = _proj_out(o_mla.reshape(n, HW), o_sb.reshape(n, HW), p["w_out_a"], p["w_out_b"], x)
    h4 = lambda a: a.reshape(b, t, N_HEADS, HEAD_DIM)
    return x, (ckv.reshape(b, t, KV_RANK), kr.reshape(b, t, ROPE_DIM), h4(skf), h4(svf))


def _tail_rows(a, n):
    t = a.shape[1]
    if t >= n:
        return a[:, t - n:]
    return jnp.pad(a, ((0, 0), (n - t, 0)) + ((0, 0),) * (a.ndim - 2))


def _cd_mixer(x, b, t, past, band_len, p):
    n = x.shape[0]
    proj = _proj_in(x, p["g_mix"], p["w_in"], 1280)
    pad_rows = BAND_LEFT * CHUNK if past is None else 0
    bq, bkf, bkb, bvf, bvb, fq, fkf, fkb, fvf, fvb, lf = _cd_post(proj, p["gains"], p["f_bias"], b, t, pad_rows)
    r3 = lambda a: a.reshape(b, -1, a.shape[-1])
    lf3 = r3(lf)
    if past is None:
        o_band = _band_prompt(r3(bq), r3(bkb), r3(bvb), p["rel_bias"])
        buf_k, buf_v = _tail_rows(r3(bkf), band_len), _tail_rows(r3(bvf), band_len)
        cum = _cumsum_rows(lf3, LOG2E)
        o_fox = _flash(r3(fq), r3(fkb), r3(fvb), "causal", jnp.swapaxes(cum, 1, 2), cum)
    else:
        c_bk, c_bv, c_fk, c_fv, c_lf = past
        pl_ = c_fk.shape[1]
        bl = c_bk.shape[1]
        o_band = _step_softmax("band", r3(bq), c_bk.reshape(b, bl, HW), c_bv.reshape(b, bl, HW), r3(bkb), r3(bvb),
                               p["rel_bias"][:, :, :bl], p["rel_bias"][:, :, bl:])
        buf_k = _tail_rows(jnp.concatenate([c_bk.reshape(b, bl, HW), r3(bkf)], 1), band_len)
        buf_v = _tail_rows(jnp.concatenate([c_bv.reshape(b, bl, HW), r3(bvf)], 1), band_len)
        cum = _cumsum_rows(jnp.concatenate([c_lf.astype(F32), lf3], 1), LOG2E)
        o_fox = _step_softmax("fox", r3(fq), c_fk.reshape(b, pl_, HW), c_fv.reshape(b, pl_, HW), r3(fkb), r3(fvb),
                              jnp.swapaxes(cum[:, :, pl_:], 1, 2), cum[:, :, :pl_], cum[:, :, pl_:])
    x = _proj_out(o_band.reshape(n, HW), o_fox.reshape(n, HW), p["w_out_a"], p["w_out_b"], x)
    h4 = lambda a: a.reshape(a.shape[0], a.shape[1], N_HEADS, HEAD_DIM)
    return x, (h4(buf_k), h4(buf_v), h4(r3(fkf)), h4(r3(fvf)), lf3)


def _trunk(x3, past_len, caches, band_len, layers, ffn_w):
    b, t, d = x3.shape
    x = x3.reshape(b * t, d)
    pos = jnp.arange(past_len, past_len + t)
    ab_states, cd_states = [], []
    for l, p in enumerate(layers):
        i = l // 2
        x = _ffn(x, p["g_ffn1"], *ffn_w, l, 0)
        if l % 2 == 0:
            past = None if caches is None else tuple(c[i] for c in caches[:4])
            x, st = _ab_mixer(x, b, t, pos, past, p)
            ab_states.append(st)
        else:
            past = None if caches is None else tuple(c[i] for c in caches[4:])
            x, st = _cd_mixer(x, b, t, past, band_len, p)
            cd_states.append(st)
        x = _ffn(x, p["g_ffn2"], *ffn_w, l, 1)
    states = [jnp.stack(s, 0) for s in zip(*ab_states)] + [jnp.stack(s, 0) for s in zip(*cd_states)]
    return x.reshape(b, t, d), states


def kernel(x_prompt, x_sample, cache_mla_ckv, cache_mla_krope, cache_sb_k, cache_sb_v, cache_band_k,
           cache_band_v, cache_fox_k, cache_fox_v, cache_fox_logf, norm_gain, ffn_w_gate, ffn_w_up,
           ffn_w_down, ab_w_in, mla_lat_gain, mla_w_uq, mla_w_ukv, mla_qk_gain, ab_w_out, cd_w_in,
           fox_f_bias, cd_qk_gain, band_rel_bias, cd_w_out):
    depth = norm_gain.shape[0]
    past_len = cache_fox_k.shape[2]
    band_len = cache_band_k.shape[2]
    t_step = x_sample.shape[1]
    assert t_step == CHUNK and past_len % CHUNK == 0 and band_len == BAND_LEFT * CHUNK
    assert x_prompt.shape[1] % CHUNK == 0

    layers = []
    for l in range(depth):
        i = l // 2
        if l % 2 == 0:
            p = _prep_ab(ab_w_in[i], mla_lat_gain[i], mla_w_uq[i], mla_w_ukv[i], mla_qk_gain[i], ab_w_out[i])
        else:
            p = _prep_cd(cd_w_in[i], fox_f_bias[i], cd_qk_gain[i], cd_w_out[i])
            p["rel_bias"] = _rel_bias(band_rel_bias[i])
        g = norm_gain[l]
        p["g_ffn1"], p["g_mix"], p["g_ffn2"] = g[0:1], g[1:2], g[2:3]
        layers.append(p)
    ffn_w = tuple(_to_bf16(w) for w in (ffn_w_gate, ffn_w_up, ffn_w_down))

    caches = (cache_mla_ckv, cache_mla_krope, cache_sb_k, cache_sb_v,
              cache_band_k, cache_band_v, cache_fox_k, cache_fox_v, cache_fox_logf)
    y_p, st_p = _trunk(x_prompt, 0, None, band_len, layers, ffn_w)
    y_s, st_s = _trunk(x_sample, past_len, caches, band_len, layers, ffn_w)
    out = [y_p, y_s]
    for a, c in zip(st_p, st_s):
        out += [a, c]
    return tuple(out)
```

```python
import functools

import jax
import jax.numpy as jnp
from jax import lax
from jax.experimental import pallas as pl
from jax.experimental.pallas import tpu as pltpu

F32 = jnp.float32
BF16 = jnp.bfloat16

EPS = 1e-6
NEG_INF = -1e30
CHUNK = 64
BAND_LEFT = 8
REL_CLIP = 128
N_HEADS = 8
HEAD_DIM = 128
Q_RANK = 512
KV_RANK = 256
ROPE_DIM = 64
NOPE_DIM = 128
QK_HEAD = NOPE_DIM + ROPE_DIM
MLA_QK_PAD = 256
ROPE_THETA = 10000.0
MLA_SCALE = QK_HEAD ** -0.5
ATT_SCALE = HEAD_DIM ** -0.5
LOG2E = 1.4426950408889634
HEAD_PAIR = 2
SB_HEADS = 2
STEP_SB_GROUP = 2
HW = N_HEADS * HEAD_DIM
AB_COLS = 4096
CD_COLS = 6400
BAND_KEYS = (BAND_LEFT + 1) * CHUNK

VMEM_LIMIT_BYTES = 48 * 1024 * 1024
FFN_VMEM_LIMIT_BYTES = 56 * 1024 * 1024


def _params(*sem):
    return pltpu.CompilerParams(dimension_semantics=sem, vmem_limit_bytes=VMEM_LIMIT_BYTES)


def _rms(x, gain, inv_n=None):
    if inv_n is None:
        ms = jnp.mean(x * x, axis=-1, keepdims=True)
    else:
        ms = jnp.sum(x * x, axis=-1, keepdims=True) * inv_n
    return x * lax.rsqrt(ms + EPS) * gain


def _dot(a, b):
    return jnp.dot(a, b, preferred_element_type=F32)


def _lane_tile(x, n):
    return x if n == 1 else jnp.concatenate([x] * n, axis=1)


def _dot_t(a, b):
    return lax.dot_general(a, b, (((1,), (1,)), ((), ())), preferred_element_type=F32)


def _ffn_kernel(x_ref, g_ref, wg_ref, wu_ref, wd_ref, o_ref, h_ref, *, n_f):
    j = pl.program_id(1)

    @pl.when(j == 0)
    def _():
        h_ref[...] = _rms(x_ref[...], g_ref[...]).astype(BF16)
        o_ref[...] = jnp.zeros_like(o_ref)

    h = h_ref[...]
    g = _dot(h, wg_ref[...])
    u = _dot(h, wu_ref[...])
    a = (g / (1.0 + jnp.exp(-g)) * u).astype(BF16)
    o_ref[...] += _dot(a, wd_ref[...])

    @pl.when(j == n_f - 1)
    def _():
        o_ref[...] = x_ref[...] + 0.5 * o_ref[...]


def _ffn(x, gain, wg, wu, wd, l, s):
    n, d = x.shape
    f = wg.shape[-1]
    tm = min(1024, n)
    tf = 512 if f % 512 == 0 else f
    n_f = f // tf
    return pl.pallas_call(
        functools.partial(_ffn_kernel, n_f=n_f),
        grid=(n // tm, n_f),
        in_specs=[
            pl.BlockSpec((tm, d), lambda i, j: (i, 0)),
            pl.BlockSpec((1, d), lambda i, j: (0, 0)),
            pl.BlockSpec((None, None, d, tf), lambda i, j: (l, s, 0, j)),
            pl.BlockSpec((None, None, d, tf), lambda i, j: (l, s, 0, j)),
            pl.BlockSpec((None, None, tf, d), lambda i, j: (l, s, j, 0)),
        ],
        out_specs=pl.BlockSpec((tm, d), lambda i, j: (i, 0)),
        out_shape=jax.ShapeDtypeStruct((n, d), F32),
        scratch_shapes=[pltpu.VMEM((tm, d), BF16)],
        compiler_params=pltpu.CompilerParams(dimension_semantics=("parallel", "arbitrary"),
                                             vmem_limit_bytes=FFN_VMEM_LIMIT_BYTES),
        name="ffn",
    )(x, gain, wg, wu, wd)


def _cast_kernel(x_ref, o_ref):
    o_ref[...] = x_ref[...].astype(BF16)


def _to_bf16(w):
    shape = w.shape
    w2 = w.reshape(-1, shape[-1])
    r, c = w2.shape
    tr = 256 if r % 256 == 0 else r
    out = pl.pallas_call(
        _cast_kernel,
        grid=(r // tr,),
        in_specs=[pl.BlockSpec((tr, c), lambda i: (i, 0))],
        out_specs=pl.BlockSpec((tr, c), lambda i: (i, 0)),
        out_shape=jax.ShapeDtypeStruct((r, c), BF16),
        compiler_params=_params("parallel"),
        name="to_bf16",
    )(w2)
    return out.reshape(shape)


def _proj_in_kernel(x_ref, g_ref, w_ref, o_ref, h_ref):
    @pl.when(pl.program_id(1) == 0)
    def _():
        h_ref[...] = _rms(x_ref[...], g_ref[...]).astype(BF16)

    o_ref[...] = _dot(h_ref[...], w_ref[...])


def _proj_in(x, gain, w, tn):
    n, d = x.shape
    c = w.shape[1]
    tm = min(1024, n)
    return pl.pallas_call(
        _proj_in_kernel,
        grid=(n // tm, c // tn),
        in_specs=[
            pl.BlockSpec((tm, d), lambda i, j: (i, 0)),
            pl.BlockSpec((1, d), lambda i, j: (0, 0)),
            pl.BlockSpec((d, tn), lambda i, j: (0, j)),
        ],
        out_specs=pl.BlockSpec((tm, tn), lambda i, j: (i, j)),
        out_shape=jax.ShapeDtypeStruct((n, c), F32),
        scratch_shapes=[pltpu.VMEM((tm, d), BF16)],
        compiler_params=_params("parallel", "arbitrary"),
        name="proj_in",
    )(x, gain, w)


def _proj_out_kernel(a_ref, b_ref, wa_ref, wb_ref, x_ref, o_ref):
    o_ref[...] = x_ref[...] + _dot(a_ref[...], wa_ref[...]) + _dot(b_ref[...], wb_ref[...])


def _proj_out(oa, ob, wa, wb, x):
    n, d = x.shape
    k = oa.shape[1]
    tm = min(1024, n)
    tn = min(1024, d)
    return pl.pallas_call(
        _proj_out_kernel,
        grid=(n // tm, d // tn),
        in_specs=[
            pl.BlockSpec((tm, k), lambda i, j: (i, 0)),
            pl.BlockSpec((tm, k), lambda i, j: (i, 0)),
            pl.BlockSpec((k, tn), lambda i, j: (0, j)),
            pl.BlockSpec((k, tn), lambda i, j: (0, j)),
            pl.BlockSpec((tm, tn), lambda i, j: (i, j)),
        ],
        out_specs=pl.BlockSpec((tm, tn), lambda i, j: (i, j)),
        out_shape=jax.ShapeDtypeStruct((n, d), F32),
        compiler_params=_params("parallel", "parallel"),
        name="proj_out",
    )(oa, ob, wa, wb, x)


def _rope_slab(x, cs):
    c = cs[:, 0:128]
    s1 = cs[:, 128:256]
    s2 = cs[:, 256:384]
    return x * c + pltpu.roll(x, 96, 1) * s1 + pltpu.roll(x, 32, 1) * s2


def _store_heads(dst_ref, x):
    m = x.shape[0]
    for h in range(N_HEADS):
        dst_ref[pl.ds(h, m, stride=N_HEADS), :] = x[:, h * HEAD_DIM:(h + 1) * HEAD_DIM]


def _ab_post_kernel(sq_ref, sk_ref, sv_ref, tail_ref, cs_ref, gn_ref, wuq_ref, *rest):
    q_ref, ckv_ref, kr_ref, sqb_ref, skb_ref, svb_ref, skf_ref, svf_ref = rest[4:]
    sqb_ref[...] = (sq_ref[...] * ATT_SCALE).astype(BF16)
    sk = sk_ref[...]
    sv = sv_ref[...]
    _store_heads(skf_ref, sk)
    _store_heads(svf_ref, sv)
    skb_ref[...] = sk.astype(BF16)
    svb_ref[...] = sv.astype(BF16)

    cs = cs_ref[...]
    g_cq = gn_ref[0:1, :]
    g_ckv = gn_ref[1:2, 0:KV_RANK]
    gq_n = gn_ref[2:3, 0:128]
    gq_r = gn_ref[3:4, 0:128]
    gk_r = gn_ref[4:5, 0:128]

    cq = _rms(tail_ref[:, 0:Q_RANK], g_cq)
    ckv_ref[...] = _rms(tail_ref[:, Q_RANK:Q_RANK + KV_RANK], g_ckv)
    kr = _rms(tail_ref[:, 768:896], gk_r, 1.0 / ROPE_DIM)
    kr_ref[...] = _rope_slab(kr, cs)[:, 0:ROPE_DIM]

    q = _dot(cq.astype(BF16), wuq_ref[...])
    for h in range(N_HEADS):
        lo = h * MLA_QK_PAD
        qn = _rms(q[:, lo:lo + 128], gq_n)
        qr = _rope_slab(_rms(q[:, lo + 128:lo + 256], gq_r, 1.0 / ROPE_DIM), cs)
        q_ref[:, lo:lo + 128] = (qn * (MLA_SCALE * LOG2E)).astype(BF16)
        q_ref[:, lo + 128:lo + 256] = (qr * (MLA_SCALE * LOG2E)).astype(BF16)


def _ab_post(proj, cs_tab, gains, wuq, layer, stacks):
    n = proj.shape[0]
    tm = min(256, n, cs_tab.shape[0])
    n_tab = cs_tab.shape[0] // tm
    row = lambda i: (i, 0)
    slot = lambda i: (layer, i, 0)
    any_spec = pl.BlockSpec(memory_space=pl.ANY)
    b16 = lambda w: jax.ShapeDtypeStruct((n, w), BF16)
    same = lambda a: jax.ShapeDtypeStruct(a.shape, a.dtype)
    return pl.pallas_call(
        _ab_post_kernel,
        grid=(n // tm,),
        in_specs=[
            pl.BlockSpec((tm, HW), lambda i: (i, 0)),
            pl.BlockSpec((tm, HW), lambda i: (i, 1)),
            pl.BlockSpec((tm, HW), lambda i: (i, 2)),
            pl.BlockSpec((tm, HW), lambda i: (i, 3)),
            pl.BlockSpec((tm, 384), lambda i: (i % n_tab, 0)),
            pl.BlockSpec(gains.shape, lambda i: (0, 0)),
            pl.BlockSpec(wuq.shape, lambda i: (0, 0)),
            any_spec, any_spec, any_spec, any_spec,
        ],
        out_specs=[
            pl.BlockSpec((tm, N_HEADS * MLA_QK_PAD), row),
            pl.BlockSpec((None, tm, KV_RANK), slot),
            pl.BlockSpec((None, tm, ROPE_DIM), slot),
            pl.BlockSpec((tm, HW), row),
            pl.BlockSpec((tm, HW), row),
            pl.BlockSpec((tm, HW), row),
            pl.BlockSpec((None, tm * N_HEADS, HEAD_DIM), slot),
            pl.BlockSpec((None, tm * N_HEADS, HEAD_DIM), slot),
        ],
        out_shape=[b16(N_HEADS * MLA_QK_PAD), same(stacks[0]), same(stacks[1]), b16(HW), b16(HW), b16(HW),
                   same(stacks[2]), same(stacks[3])],
        input_output_aliases={7: 1, 8: 2, 9: 6, 10: 7},
        compiler_params=_params("parallel"),
        name="ab_post",
    )(proj, proj, proj, proj, cs_tab, gains, wuq, *stacks)


def _mla_kv_kernel(ckv_ref, kr_ref, w_ref, g_ref, k_ref, v_ref):
    kv = _dot(ckv_ref[...].astype(BF16), w_ref[...])
    m = kv.shape[0]
    kr = jnp.concatenate([kr_ref[...], jnp.zeros((m, 128 - ROPE_DIM), F32)], axis=1).astype(BF16)
    g = g_ref[...]
    for h in range(N_HEADS):
        lo = h * 256
        k_ref[:, lo:lo + 128] = _rms(kv[:, lo:lo + 128], g).astype(BF16)
        k_ref[:, lo + 128:lo + 256] = kr
        v_ref[:, h * 128:(h + 1) * 128] = kv[:, lo + 128:lo + 256].astype(BF16)


def _mla_kv(ckv, kr, layer, w_ukv, gk_n):
    m = ckv.shape[1]
    tm = min(512, m)
    row = lambda i: (i, 0)
    return pl.pallas_call(
        _mla_kv_kernel,
        grid=(m // tm,),
        in_specs=[
            pl.BlockSpec((None, tm, KV_RANK), lambda i: (layer, i, 0)),
            pl.BlockSpec((None, tm, ROPE_DIM), lambda i: (layer, i, 0)),
            pl.BlockSpec(w_ukv.shape, lambda i: (0, 0)),
            pl.BlockSpec((1, 128), lambda i: (0, 0)),
        ],
        out_specs=[
            pl.BlockSpec((tm, N_HEADS * MLA_QK_PAD), row),
            pl.BlockSpec((tm, HW), row),
        ],
        out_shape=[
            jax.ShapeDtypeStruct((m, N_HEADS * MLA_QK_PAD), BF16),
            jax.ShapeDtypeStruct((m, HW), BF16),
        ],
        compiler_params=_params("parallel"),
        name="mla_kv",
    )(ckv, kr, w_ukv, gk_n)


def _cd_post_kernel(bq_ref, bk_ref, bv_ref, fq_ref, fk_ref, fv_ref, fl_ref, gn_ref, fb_ref, *rest, pad_tiles):
    bqb_ref, bkb_ref, bvb_ref, fqb_ref, fkb_ref, fvb_ref, bkt_ref, bvt_ref, fks_ref, fvs_ref, lf_ref = rest[5:]
    live = pl.program_id(1) >= pad_tiles
    m = bq_ref.shape[0]
    bv = bv_ref[...]
    _store_heads(bvt_ref, bv)
    bvb_ref[...] = jnp.where(live, bv, 0.0).astype(BF16)
    fv = fv_ref[...]
    _store_heads(fvs_ref, fv)
    fvb_ref[...] = fv.astype(BF16)
    for h in range(N_HEADS):
        sl = slice(h * 128, (h + 1) * 128)
        rows = pl.ds(h, m, stride=N_HEADS)
        bqb_ref[:, sl] = (_rms(bq_ref[:, sl], gn_ref[0:1, :]) * (ATT_SCALE * LOG2E)).astype(BF16)
        bk = _rms(bk_ref[:, sl], gn_ref[1:2, :])
        bkt_ref[rows, :] = bk
        bkb_ref[:, sl] = jnp.where(live, bk, 0.0).astype(BF16)
        fqb_ref[:, sl] = (_rms(fq_ref[:, sl], gn_ref[2:3, :]) * (ATT_SCALE * LOG2E)).astype(BF16)
        fk = _rms(fk_ref[:, sl], gn_ref[3:4, :])
        fks_ref[rows, :] = fk
        fkb_ref[:, sl] = fk.astype(BF16)
    z = fl_ref[:, 0:128] + fb_ref[...]
    lf = -(jnp.maximum(-z, 0.0) + jnp.log1p(jnp.exp(-jnp.abs(z))))
    lf_ref[...] = lf[:, 0:N_HEADS]


def _cd_post(proj, gains, f_bias, b, t, pad_rows, layer, tail_slot, tail_rows, states):
    n = proj.shape[0]
    tm = min(256, t)
    tpb = t // tm
    pad_tiles = pad_rows // tm
    tail_tiles = tail_rows // tm
    tile = lambda s: jnp.maximum(s - pad_tiles, 0)
    src = lambda bi, s: bi * tpb + tile(s)
    tail = lambda bi, s: (tail_slot, bi * tail_tiles + jnp.maximum(tile(s) - (tpb - tail_tiles), 0), 0)
    col = lambda c: pl.BlockSpec((tm, HW), lambda bi, s, c=c: (src(bi, s), c))
    b16o = jax.ShapeDtypeStruct((n, HW), BF16)
    padded = jax.ShapeDtypeStruct((b * (pad_rows + t), HW), BF16)
    blk = pl.BlockSpec((tm, HW), lambda bi, s: (src(bi, s), 0))
    pblk = pl.BlockSpec((tm, HW), lambda bi, s: (bi * (tpb + pad_tiles) + s, 0))
    heads_blk = lambda imap: pl.BlockSpec((None, tm * N_HEADS, HEAD_DIM), imap)
    slot = lambda bi, s: (layer, src(bi, s), 0)
    any_spec = pl.BlockSpec(memory_space=pl.ANY)
    same = lambda a: jax.ShapeDtypeStruct(a.shape, a.dtype)
    return pl.pallas_call(
        functools.partial(_cd_post_kernel, pad_tiles=pad_tiles),
        grid=(b, tpb + pad_tiles),
        in_specs=[col(0), col(1), col(2), col(3), col(4), col(5),
                  pl.BlockSpec((tm, 256), lambda bi, s: (src(bi, s), 6 * HW // 256)),
                  pl.BlockSpec((4, 128), lambda bi, s: (0, 0)),
                  pl.BlockSpec((1, 128), lambda bi, s: (0, 0))] + [any_spec] * 5,
        out_specs=[blk, pblk, pblk, blk, blk, blk,
                   heads_blk(tail), heads_blk(tail), heads_blk(slot), heads_blk(slot),
                   pl.BlockSpec((None, tm, N_HEADS), slot)],
        out_shape=[b16o, padded, padded, b16o, b16o, b16o] + [same(a) for a in states],
        input_output_aliases={9: 6, 10: 7, 11: 8, 12: 9, 13: 10},
        compiler_params=_params("arbitrary", "arbitrary"),
        name="cd_post",
    )(proj, proj, proj, proj, proj, proj, proj, gains, f_bias, *states)


def _band_buf_kernel(cache_ref, new_ref, _, o_ref):
    keep = cache_ref.shape[0] - new_ref.shape[0]
    o_ref[0:keep, :] = cache_ref[new_ref.shape[0]:, :]
    o_ref[keep:, :] = new_ref[...]


def _band_buf(cache, new, layer, out):
    _, b, rows, _ = cache.shape
    blk = pl.BlockSpec((None, None, rows, HEAD_DIM), lambda i: (layer, i, 0, 0))
    return pl.pallas_call(
        _band_buf_kernel,
        grid=(b,),
        in_specs=[blk, pl.BlockSpec((None, new.shape[1], HEAD_DIM), lambda i: (i, 0, 0)),
                  pl.BlockSpec(memory_space=pl.ANY)],
        out_specs=blk,
        out_shape=jax.ShapeDtypeStruct(out.shape, out.dtype),
        input_output_aliases={2: 0},
        compiler_params=_params("parallel"),
        name="band_buf",
    )(cache, new, out)


def _cumsum_kernel(x_ref, o_ref, *, n_blk, scale):
    jj = lax.broadcasted_iota(jnp.int32, (128, 128), 0)
    ss = lax.broadcasted_iota(jnp.int32, (128, 128), 1)
    upper = jnp.where(jj <= ss, 1.0, 0.0).astype(BF16)

    def body(i, carry):
        x = x_ref[i]
        hi = x.astype(BF16)
        r1 = x - hi.astype(F32)
        mid = r1.astype(BF16)
        lo = (r1 - mid.astype(F32)).astype(BF16)
        c = _dot(hi, upper) + _dot(mid, upper) + _dot(lo, upper) + carry
        o_ref[i] = c * scale
        return c[:, 127:128]

    lax.fori_loop(0, n_blk, body, jnp.zeros((N_HEADS, 1), F32))


def _cumsum_rows(lf, scale):
    b, length, nh = lf.shape
    n_blk = -(-length // 128)
    x = jnp.pad(jnp.swapaxes(lf, 1, 2), ((0, 0), (0, 0), (0, n_blk * 128 - length)))
    x = jnp.swapaxes(x.reshape(b, nh, n_blk, 128), 1, 2)
    out = pl.pallas_call(
        functools.partial(_cumsum_kernel, n_blk=n_blk, scale=scale),
        grid=(b,),
        in_specs=[pl.BlockSpec((None, n_blk, nh, 128), lambda i: (i, 0, 0, 0))],
        out_specs=pl.BlockSpec((None, n_blk, nh, 128), lambda i: (i, 0, 0, 0)),
        out_shape=jax.ShapeDtypeStruct((b, n_blk, nh, 128), F32),
        compiler_params=_params("parallel"),
        name="cumsum",
    )(x)
    return jnp.swapaxes(out, 1, 2).reshape(b, nh, n_blk * 128)[:, :, :length]


def _rel_bias_kernel(tab_ref, o_ref, *, n_tab):
    h = pl.program_id(0)
    i = lax.broadcasted_iota(jnp.int32, (CHUNK, BAND_KEYS), 0)
    j = lax.broadcasted_iota(jnp.int32, (CHUNK, BAND_KEYS), 1)
    idx = jnp.clip(BAND_LEFT * CHUNK + i - j, -REL_CLIP, REL_CLIP) + REL_CLIP

    def body(r, acc):
        return jnp.where(idx == r, tab_ref[h, r] * LOG2E, acc)

    o_ref[...] = lax.fori_loop(0, n_tab, body, jnp.zeros((CHUNK, BAND_KEYS), F32))


def _rel_bias(table):
    nh, n_tab = table.shape
    return pl.pallas_call(
        functools.partial(_rel_bias_kernel, n_tab=n_tab),
        grid=(nh,),
        in_specs=[pl.BlockSpec(memory_space=pltpu.SMEM)],
        out_specs=pl.BlockSpec((None, CHUNK, BAND_KEYS), lambda h: (h, 0, 0)),
        out_shape=jax.ShapeDtypeStruct((nh, CHUNK, BAND_KEYS), F32),
        compiler_params=_params("arbitrary"),
        name="rel_bias",
    )(table)


def _flash_kernel(*refs, tq, mode, has_bias):
    if has_bias:
        q_ref, k_ref, v_ref, cq_ref, ck_ref, o_ref, m_scr, acc_scr = refs
    else:
        q_ref, k_ref, v_ref, o_ref, m_scr, acc_scr = refs
    hg = pl.program_id(1)
    qi = pl.program_id(2)
    dk = q_ref.shape[1] // HEAD_PAIR
    reps = tq // HEAD_DIM
    m_scr[...] = jnp.full(m_scr.shape, NEG_INF, F32)
    acc_scr[...] = jnp.zeros_like(acc_scr)
    ones = jnp.ones((tq, HEAD_DIM), BF16)
    if has_bias:
        lane = lax.broadcasted_iota(jnp.int32, cq_ref.shape, 1)
        cqs = [jnp.broadcast_to(
            jnp.sum(jnp.where(lane == hg * HEAD_PAIR + j, cq_ref[...], 0.0), axis=-1, keepdims=True),
            (tq, HEAD_DIM)) for j in range(HEAD_PAIR)]

    def step(kb, masked):
        ks = pl.multiple_of(kb * tq, tq)
        scores = [_dot_t(q_ref[:, j * dk:(j + 1) * dk], k_ref[pl.ds(ks, tq), j * dk:(j + 1) * dk])
                  for j in range(HEAD_PAIR)]
        for j in range(HEAD_PAIR):
            s = scores[j]
            if has_bias:
                s = s - ck_ref[j, pl.ds(kb, 1), :]
            if masked:
                row = lax.broadcasted_iota(jnp.int32, (tq, tq), 0)
                col = lax.broadcasted_iota(jnp.int32, (tq, tq), 1)
                if mode == "chunk":
                    ok = lax.shift_right_logical(col, 6) <= lax.shift_right_logical(row, 6)
                else:
                    ok = col <= row
                s = jnp.where(ok, s, NEG_INF)
            m_old = m_scr[j]
            rmax = jnp.max(s, axis=-1, keepdims=True)
            if has_bias:
                m_new = jnp.maximum(m_old, rmax + cqs[j])
                shift = m_new - cqs[j]
            else:
                m_new = jnp.maximum(m_old, rmax)
                shift = m_new
            p = jnp.exp2(s - _lane_tile(shift, reps))
            alpha = jnp.exp2(m_old - m_new)
            v1 = jnp.concatenate([v_ref[pl.ds(ks, tq), j * HEAD_DIM:(j + 1) * HEAD_DIM], ones], axis=1)
            acc_scr[j] = _lane_tile(alpha, 2) * acc_scr[j] + _dot(p.astype(BF16), v1)
            m_scr[j] = m_new

    def body(kb, c):
        step(kb, False)
        return c

    lax.fori_loop(0, qi, body, 0)
    step(qi, True)
    for j in range(HEAD_PAIR):
        acc = acc_scr[j]
        o_ref[:, j * HEAD_DIM:(j + 1) * HEAD_DIM] = (acc[:, :HEAD_DIM] / acc[:, HEAD_DIM:]).astype(BF16)


def _flash(q, k, v, mode, cq=None, ck=None):
    b, t, _ = q.shape
    dk = q.shape[2] // N_HEADS
    tq = min(512, t)
    nq = t // tq
    has_bias = cq is not None
    hp = HEAD_PAIR
    in_specs = [
        pl.BlockSpec((None, tq, hp * dk), lambda bi, h, qi: (bi, qi, h)),
        pl.BlockSpec((None, t, hp * dk), lambda bi, h, qi: (bi, 0, h)),
        pl.BlockSpec((None, t, hp * HEAD_DIM), lambda bi, h, qi: (bi, 0, h)),
    ]
    args = [q, k, v]
    if has_bias:
        in_specs += [
            pl.BlockSpec((None, tq, N_HEADS), lambda bi, h, qi: (bi, qi, 0)),
            pl.BlockSpec((None, hp, nq, tq), lambda bi, h, qi: (bi, h, 0, 0)),
        ]
        args += [cq, ck.reshape(b, N_HEADS, nq, tq)]
    return pl.pallas_call(
        functools.partial(_flash_kernel, tq=tq, mode=mode, has_bias=has_bias),
        grid=(b, N_HEADS // hp, nq),
        in_specs=in_specs,
        out_specs=pl.BlockSpec((None, tq, hp * HEAD_DIM), lambda bi, h, qi: (bi, qi, h)),
        out_shape=jax.ShapeDtypeStruct((b, t, HW), BF16),
        scratch_shapes=[pltpu.VMEM((hp, tq, HEAD_DIM), F32), pltpu.VMEM((hp, tq, 2 * HEAD_DIM), F32)],
        compiler_params=_params("parallel", "parallel", "arbitrary"),
        name="flash_" + mode,
    )(*args)


def _sb_stage1(z, tri, ok):
    n = z.shape[1]
    sp = jnp.maximum(z, 0.0) + jnp.log(1.0 + jnp.exp(-jnp.abs(z)))
    lsig = z - sp
    if ok is not None:
        sp = jnp.where(ok, sp, 0.0)
    hi = sp.astype(BF16)
    lo = (sp - hi.astype(F32)).astype(BF16)
    if n % HEAD_DIM == 0:
        right = _dot(jnp.concatenate([hi, lo], axis=1), tri)
    else:
        right = _dot(hi, tri[:n]) + _dot(lo, tri[:n])
    return lsig, right, jnp.sum(sp, axis=-1, keepdims=True)


def _sb_stage2(lsig, right, c, v, ok):
    n = lsig.shape[1]
    c_wide = _lane_tile(c, n // HEAD_DIM) if n % HEAD_DIM == 0 else c[:, :n]
    w = jnp.exp(lsig - right + c_wide)
    if ok is not None:
        w = jnp.where(ok, w, 0.0)
    return _dot(w.astype(BF16), v)


def _tri(n):
    jj = lax.broadcasted_iota(jnp.int32, (2 * n, n), 0)
    ss = lax.broadcasted_iota(jnp.int32, (2 * n, n), 1)
    return jnp.where(jnp.where(jj >= n, jj - n, jj) > ss, 1.0, 0.0).astype(BF16)


def _sb_kernel(q_ref, k_ref, v_ref, o_ref, c_scr, acc_scr, *, tq, sub):
    qi = pl.program_id(2)
    n_sub = tq // sub
    tri = _tri(sub)
    c_scr[...] = jnp.zeros_like(c_scr)
    acc_scr[...] = jnp.zeros_like(acc_scr)

    def block(kb, diag):
        chains = [(d, j) for d in reversed(range(n_sub)) for j in range(SB_HEADS)]
        hs = lambda j: slice(j * HEAD_DIM, (j + 1) * HEAD_DIM)
        ks = lambda d: pl.multiple_of(kb * tq + d * sub, sub)
        oks = {}
        for d in range(n_sub):
            oks[d] = None
            if diag:
                row = lax.broadcasted_iota(jnp.int32, (tq, sub), 0)
                col = lax.broadcasted_iota(jnp.int32, (tq, sub), 1)
                oks[d] = col + d * sub < row
        zs = {(d, j): _dot_t(q_ref[:, hs(j)], k_ref[pl.ds(ks(d), sub), hs(j)]) for d, j in chains}
        mids = {dj: _sb_stage1(zs[dj], tri, oks[dj[0]]) for dj in chains}
        for d, j in chains:
            lsig, right, rowsum = mids[(d, j)]
            c = c_scr[j]
            acc_scr[j] += _sb_stage2(lsig, right, c, v_ref[pl.ds(ks(d), sub), hs(j)], oks[d])
            c_scr[j] = c - rowsum

    block(qi, True)

    def body(i, c):
        block(qi - 1 - i, False)
        return c

    lax.fori_loop(0, qi, body, 0)
    for j in range(SB_HEADS):
        o_ref[:, j * HEAD_DIM:(j + 1) * HEAD_DIM] = acc_scr[j].astype(BF16)


def _sb_prompt(q, k, v):
    b, t, _ = q.shape
    tq = min(512, t)
    sub = min(256, tq)
    hp = SB_HEADS
    return pl.pallas_call(
        functools.partial(_sb_kernel, tq=tq, sub=sub),
        grid=(b, N_HEADS // hp, t // tq),
        in_specs=[
            pl.BlockSpec((None, tq, hp * HEAD_DIM), lambda bi, h, qi: (bi, qi, h)),
            pl.BlockSpec((None, t, hp * HEAD_DIM), lambda bi, h, qi: (bi, 0, h)),
            pl.BlockSpec((None, t, hp * HEAD_DIM), lambda bi, h, qi: (bi, 0, h)),
        ],
        out_specs=pl.BlockSpec((None, tq, hp * HEAD_DIM), lambda bi, h, qi: (bi, qi, h)),
        out_shape=jax.ShapeDtypeStruct((b, t, HW), BF16),
        scratch_shapes=[pltpu.VMEM((hp, tq, HEAD_DIM), F32), pltpu.VMEM((hp, tq, HEAD_DIM), F32)],
        compiler_params=_params("parallel", "parallel", "arbitrary"),
        name="sb_prompt",
    )(q, k, v)


def _band_kernel(q_ref, k_ref, v_ref, b_ref, o_ref, *, n_chunks, unroll):
    col = lax.broadcasted_iota(jnp.int32, (CHUNK, BAND_KEYS), 1)
    ones = jnp.ones((BAND_KEYS, HEAD_DIM), BF16)

    def body(i, carry):
        chains = [(u, j) for u in range(unroll) for j in range(HEAD_PAIR)]
        hs = lambda j: slice(j * HEAD_DIM, (j + 1) * HEAD_DIM)
        qs = lambda u: pl.multiple_of((i * unroll + u) * CHUNK, CHUNK)
        scores = {(u, j): _dot_t(q_ref[pl.ds(qs(u), CHUNK), hs(j)], k_ref[pl.ds(qs(u), BAND_KEYS), hs(j)])
                  for u, j in chains}
        for u, j in chains:
            ok = col >= (BAND_LEFT - (i * unroll + u)) * CHUNK
            s = jnp.where(ok, scores[(u, j)] + b_ref[j], NEG_INF)
            p = jnp.exp2(s - jnp.max(s, axis=-1, keepdims=True))
            r = _dot(p.astype(BF16), jnp.concatenate([v_ref[pl.ds(qs(u), BAND_KEYS), hs(j)], ones], axis=1))
            o_ref[pl.ds(qs(u), CHUNK), hs(j)] = (r[:, :HEAD_DIM] / r[:, HEAD_DIM:]).astype(BF16)
        return carry

    lax.fori_loop(0, n_chunks // unroll, body, 0)


def _band_prompt(q, k, v, bias):
    b, t, _ = q.shape
    tp = k.shape[1]
    hp = HEAD_PAIR
    n_chunks = t // CHUNK
    unroll = 4 if n_chunks % 4 == 0 else 1
    return pl.pallas_call(
        functools.partial(_band_kernel, n_chunks=n_chunks, unroll=unroll),
        grid=(b, N_HEADS // hp),
        in_specs=[
            pl.BlockSpec((None, t, hp * HEAD_DIM), lambda bi, h: (bi, 0, h)),
            pl.BlockSpec((None, tp, hp * HEAD_DIM), lambda bi, h: (bi, 0, h)),
            pl.BlockSpec((None, tp, hp * HEAD_DIM), lambda bi, h: (bi, 0, h)),
            pl.BlockSpec((hp, CHUNK, BAND_KEYS), lambda bi, h: (h, 0, 0)),
        ],
        out_specs=pl.BlockSpec((None, t, hp * HEAD_DIM), lambda bi, h: (bi, 0, h)),
        out_shape=jax.ShapeDtypeStruct((b, t, HW), BF16),
        compiler_params=_params("parallel", "parallel"),
        name="band_prompt",
    )(q, k, v, bias)


def _step_softmax_kernel(*refs, mode):
    if mode == "fox":
        q_ref, kp_ref, vp_ref, kn_ref, vn_ref, cq_ref, ckp_ref, ckn_ref, o_ref = refs
    elif mode == "band":
        q_ref, kp_ref, vp_ref, kn_ref, vn_ref, bp_ref, bn_ref, o_ref = refs
    else:
        q_ref, kp_ref, vp_ref, kn_ref, vn_ref, o_ref = refs
    t = q_ref.shape[0]
    dk = q_ref.shape[1] // N_HEADS
    if mode == "mla":
        kp_h = lambda h: kp_ref[:, h * dk:(h + 1) * dk]
        vp_h = lambda h: vp_ref[:, h * HEAD_DIM:(h + 1) * HEAD_DIM]
    else:
        p_rows = kp_ref.shape[0] // N_HEADS
        kp_h = lambda h: kp_ref[pl.ds(h, p_rows, stride=N_HEADS), :].astype(BF16)
        vp_h = lambda h: vp_ref[pl.ds(h, p_rows, stride=N_HEADS), :].astype(BF16)
    scores = []
    for h in range(N_HEADS):
        ks = slice(h * dk, (h + 1) * dk)
        scores.append((_dot_t(q_ref[:, ks], kp_h(h)), _dot_t(q_ref[:, ks], kn_ref[:, ks])))
    for h in range(N_HEADS):
        vs = slice(h * HEAD_DIM, (h + 1) * HEAD_DIM)
        sp, sn = scores[h]
        if mode == "fox":
            cq = cq_ref[:, h:h + 1]
            sp = sp + (cq - ckp_ref[h:h + 1, :])
            sn = sn + (cq - ckn_ref[h:h + 1, :])
            row = lax.broadcasted_iota(jnp.int32, (t, t), 0)
            col = lax.broadcasted_iota(jnp.int32, (t, t), 1)
            sn = jnp.where(col <= row, sn, NEG_INF)
        elif mode == "band":
            sp = sp + bp_ref[h]
            sn = sn + bn_ref[h]
        m = jnp.maximum(jnp.max(sp, axis=-1, keepdims=True), jnp.max(sn, axis=-1, keepdims=True))
        pp = jnp.exp2(sp - m)
        pn = jnp.exp2(sn - m)
        l = jnp.sum(pp, axis=-1, keepdims=True) + jnp.sum(pn, axis=-1, keepdims=True)
        o = _dot(pp.astype(BF16), vp_h(h)) + _dot(pn.astype(BF16), vn_ref[:, vs])
        o_ref[:, vs] = (o / l).astype(BF16)


def _cache_spec(cache, layer):
    return pl.BlockSpec((None, None) + cache.shape[2:], lambda i: (layer, i, 0, 0))


def _step_softmax(mode, layer, q, kp, vp, kn, vn, *extra):
    b, t, qw = q.shape
    full = lambda a: pl.BlockSpec((None,) + a.shape[1:], lambda i: (i,) + (0,) * (a.ndim - 1))
    cached = full if mode == "mla" else (lambda a: _cache_spec(a, layer))
    in_specs = [full(q), cached(kp), cached(vp), full(kn), full(vn)]
    if mode == "fox":
        in_specs += [full(e) for e in extra]
    elif mode == "band":
        in_specs += [pl.BlockSpec(e.shape, lambda i: (0, 0, 0)) for e in extra]
    return pl.pallas_call(
        functools.partial(_step_softmax_kernel, mode=mode),
        grid=(b,),
        in_specs=in_specs,
        out_specs=pl.BlockSpec((None, t, HW), lambda i: (i, 0, 0)),
        out_shape=jax.ShapeDtypeStruct((b, t, HW), BF16),
        compiler_params=_params("parallel"),
        name="step_" + mode,
    )(q, kp, vp, kn, vn, *extra)


def _step_sb_kernel(q_ref, kp_ref, vp_ref, kn_ref, vn_ref, o_ref, *, blk):
    t = q_ref.shape[0]
    p = kp_ref.shape[0] // N_HEADS
    tri_n = _tri(t)
    tri_p = _tri(blk)
    row = lax.broadcasted_iota(jnp.int32, (t, t), 0)
    col = lax.broadcasted_iota(jnp.int32, (t, t), 1)
    blocks = list(reversed(range(p // blk)))
    hsl = lambda h: slice(h * HEAD_DIM, (h + 1) * HEAD_DIM)
    rows = lambda kb, h: pl.ds(kb * blk * N_HEADS + h, blk, stride=N_HEADS)
    for h0 in range(0, N_HEADS, STEP_SB_GROUP):
        heads = range(h0, h0 + STEP_SB_GROUP)
        mids = {}
        for h in heads:
            mids[(h, None)] = _sb_stage1(_dot_t(q_ref[:, hsl(h)], kn_ref[:, hsl(h)]), tri_n, col < row)
            for kb in blocks:
                mids[(h, kb)] = _sb_stage1(_dot_t(q_ref[:, hsl(h)], kp_ref[rows(kb, h), :].astype(BF16)), tri_p, None)
        for h in heads:
            lsig, right, rowsum = mids[(h, None)]
            c = jnp.zeros((t, HEAD_DIM), F32)
            acc = _sb_stage2(lsig, right, c, vn_ref[:, hsl(h)], col < row)
            c = c - rowsum
            for kb in blocks:
                lsig, right, rowsum = mids[(h, kb)]
                acc = acc + _sb_stage2(lsig, right, c, vp_ref[rows(kb, h), :].astype(BF16), None)
                c = c - rowsum
            o_ref[:, hsl(h)] = acc.astype(BF16)


def _step_sb(layer, q, kp, vp, kn, vn):
    b, t, _ = q.shape
    p = kp.shape[2] // N_HEADS
    blk = min(256, p)
    full = lambda a: pl.BlockSpec((None,) + a.shape[1:], lambda i: (i, 0, 0))
    return pl.pallas_call(
        functools.partial(_step_sb_kernel, blk=blk),
        grid=(b,),
        in_specs=[full(q), _cache_spec(kp, layer), _cache_spec(vp, layer), full(kn), full(vn)],
        out_specs=pl.BlockSpec((None, t, HW), lambda i: (i, 0, 0)),
        out_shape=jax.ShapeDtypeStruct((b, t, HW), BF16),
        compiler_params=_params("parallel"),
        name="step_sb",
    )(q, kp, vp, kn, vn)


def _rope_tables(pos, rows):
    half = ROPE_DIM // 2
    inv = ROPE_THETA ** (-jnp.arange(half, dtype=F32) / half)
    ang = pos.astype(F32)[:, None] * inv[None, :]
    c, s, z = jnp.cos(ang), jnp.sin(ang), jnp.zeros_like(ang)
    tab = jnp.concatenate([c, c, z, z, -s, z, z, z, z, s, z, z], axis=1)
    return jnp.tile(tab, (max(1, rows // tab.shape[0]), 1))


def _prep_ab(w_in, lat_gain, w_uq, w_ukv, qk_gain, w_out):
    d = w_in.shape[0]
    cq, ckv, kr, sq, sk, sv = jnp.split(w_in, [512, 768, 832, 1856, 2880], axis=1)
    w_in_p = jnp.concatenate([sq, sk, sv, cq, ckv, kr, jnp.zeros((d, AB_COLS - w_in.shape[1]), F32)], axis=1)
    uq = w_uq.reshape(Q_RANK, N_HEADS, QK_HEAD)
    uq = jnp.pad(uq, ((0, 0), (0, 0), (0, MLA_QK_PAD - QK_HEAD))).reshape(Q_RANK, N_HEADS * MLA_QK_PAD)
    gains = jnp.zeros((8, Q_RANK), F32)
    gains = gains.at[0, :].set(lat_gain[:Q_RANK])
    gains = gains.at[1, :KV_RANK].set(lat_gain[Q_RANK:])
    gains = gains.at[2, :NOPE_DIM].set(qk_gain[0, :NOPE_DIM])
    gains = gains.at[3, :ROPE_DIM].set(qk_gain[0, NOPE_DIM:])
    gains = gains.at[4, :ROPE_DIM].set(qk_gain[1, NOPE_DIM:])
    return dict(w_in=w_in_p.astype(BF16), w_uq=uq.astype(BF16), w_ukv=w_ukv.astype(BF16), gains=gains,
                gk_n=qk_gain[1:2, :NOPE_DIM], w_out_a=w_out[:HW].astype(BF16), w_out_b=w_out[HW:].astype(BF16))


def _prep_cd(w_in, f_bias, qk_gain, w_out):
    d = w_in.shape[0]
    w_in_p = jnp.pad(w_in, ((0, 0), (0, CD_COLS - w_in.shape[1])))
    fb = jnp.pad(f_bias, (0, 128 - N_HEADS)).reshape(1, 128)
    return dict(w_in=w_in_p.astype(BF16), gains=qk_gain, f_bias=fb,
                w_out_a=w_out[:HW].astype(BF16), w_out_b=w_out[HW:].astype(BF16))


def _heads_view(c):
    l, b, p, h, d = c.shape
    return c.reshape(l, b, p * h, d)


def _ab_mixer(x, b, t, pos, i, caches, states, p):
    n = x.shape[0]
    proj = _proj_in(x, p["g_mix"], p["w_in"], 1024)
    cs = _rope_tables(pos, min(256, n))
    q, ckv, kr, sq, skb, svb, sk, sv = _ab_post(proj, cs, p["gains"], p["w_uq"], i, states)
    kf, vf = _mla_kv(ckv, kr, i, p["w_ukv"], p["gk_n"])
    r3 = lambda a: a.reshape(b, t, a.shape[-1])
    if caches is None:
        o_mla = _flash(r3(q), r3(kf), r3(vf), "chunk")
        o_sb = _sb_prompt(r3(sq), r3(skb), r3(svb))
    else:
        c_ckv, c_kr, c_sk, c_sv = caches
        na, _, pl_, _ = c_ckv.shape
        kc, vc = _mla_kv(c_ckv.reshape(na, b * pl_, KV_RANK), c_kr.reshape(na, b * pl_, ROPE_DIM), i,
                         p["w_ukv"], p["gk_n"])
        o_mla = _step_softmax("mla", i, r3(q), kc.reshape(b, pl_, -1), vc.reshape(b, pl_, -1), r3(kf), r3(vf))
        o_sb = _step_sb(i, r3(sq), _heads_view(c_sk), _heads_view(c_sv), r3(skb), r3(svb))
    x = _proj_out(o_mla.reshape(n, HW), o_sb.reshape(n, HW), p["w_out_a"], p["w_out_b"], x)
    return x, (ckv, kr, sk, sv)


def _cd_mixer(x, b, t, i, caches, band_len, states, p):
    n = x.shape[0]
    proj = _proj_in(x, p["g_mix"], p["w_in"], 1280)
    r3 = lambda a: a.reshape(b, -1, a.shape[-1])
    if caches is None:
        assert t >= band_len
        outs = _cd_post(proj, p["gains"], p["f_bias"], b, t, BAND_LEFT * CHUNK, i, i, band_len, states)
        bq, bkb, bvb, fq, fkb, fvb, bk_buf, bv_buf, fk, fv, lf = outs
        o_band = _band_prompt(r3(bq), r3(bkb), r3(bvb), p["rel_bias"])
        cum = _cumsum_rows(lf[i].reshape(b, t, N_HEADS), LOG2E)
        o_fox = _flash(r3(fq), r3(fkb), r3(fvb), "causal", jnp.swapaxes(cum, 1, 2), cum)
    else:
        c_bk, c_bv, c_fk, c_fv, c_lf = caches
        pl_ = c_fk.shape[2]
        bl = c_bk.shape[2]
        new_rows = lambda: jnp.zeros((1, n * N_HEADS, HEAD_DIM), F32)
        outs = _cd_post(proj, p["gains"], p["f_bias"], b, t, 0, i, 0, t, (new_rows(), new_rows()) + states[2:])
        bq, bkb, bvb, fq, fkb, fvb, bk_new, bv_new, fk, fv, lf = outs
        o_band = _step_softmax("band", i, r3(bq), _heads_view(c_bk), _heads_view(c_bv), r3(bkb), r3(bvb),
                               p["rel_bias"][:, :, :bl], p["rel_bias"][:, :, bl:])
        buf4 = lambda a: a.reshape(a.shape[0], b, band_len * N_HEADS, HEAD_DIM)
        bk_buf = _band_buf(_heads_view(c_bk), bk_new.reshape(b, t * N_HEADS, HEAD_DIM), i, buf4(states[0]))
        bv_buf = _band_buf(_heads_view(c_bv), bv_new.reshape(b, t * N_HEADS, HEAD_DIM), i, buf4(states[1]))
        bk_buf, bv_buf = bk_buf.reshape(states[0].shape), bv_buf.reshape(states[1].shape)
        cum = _cumsum_rows(jnp.concatenate([c_lf[i].astype(F32), lf[i].reshape(b, t, N_HEADS)], 1), LOG2E)
        o_fox = _step_softmax("fox", i, r3(fq), _heads_view(c_fk), _heads_view(c_fv), r3(fkb), r3(fvb),
                              jnp.swapaxes(cum[:, :, pl_:], 1, 2), cum[:, :, :pl_], cum[:, :, pl_:])
    x = _proj_out(o_band.reshape(n, HW), o_fox.reshape(n, HW), p["w_out_a"], p["w_out_b"], x)
    return x, (bk_buf, bv_buf, fk, fv, lf)


def _trunk(x3, past_len, caches, band_len, layers, ffn_w):
    b, t, d = x3.shape
    n = b * t
    x = x3.reshape(n, d)
    pos = jnp.arange(past_len, past_len + t)
    n_ab, n_cd = (len(layers) + 1) // 2, len(layers) // 2
    zeros = lambda *shape: jnp.zeros(shape, F32)
    ab = (zeros(n_ab, n, KV_RANK), zeros(n_ab, n, ROPE_DIM),
          zeros(n_ab, n * N_HEADS, HEAD_DIM), zeros(n_ab, n * N_HEADS, HEAD_DIM))
    cd = (zeros(n_cd, b * band_len * N_HEADS, HEAD_DIM), zeros(n_cd, b * band_len * N_HEADS, HEAD_DIM),
          zeros(n_cd, n * N_HEADS, HEAD_DIM), zeros(n_cd, n * N_HEADS, HEAD_DIM), zeros(n_cd, n, N_HEADS))
    for l, p in enumerate(layers):
        i = l // 2
        x = _ffn(x, p["g_ffn1"], *ffn_w, l, 0)
        if l % 2 == 0:
            x, ab = _ab_mixer(x, b, t, pos, i, None if caches is None else caches[:4], ab, p)
        else:
            x, cd = _cd_mixer(x, b, t, i, None if caches is None else caches[4:], band_len, cd, p)
        x = _ffn(x, p["g_ffn2"], *ffn_w, l, 1)
    h5 = lambda a, rows: a.reshape(a.shape[0], b, rows, N_HEADS, HEAD_DIM)
    states = [ab[0].reshape(n_ab, b, t, KV_RANK), ab[1].reshape(n_ab, b, t, ROPE_DIM), h5(ab[2], t), h5(ab[3], t),
              h5(cd[0], band_len), h5(cd[1], band_len), h5(cd[2], t), h5(cd[3], t), cd[4].reshape(n_cd, b, t, N_HEADS)]
    return x.reshape(b, t, d), states


def kernel(x_prompt, x_sample, cache_mla_ckv, cache_mla_krope, cache_sb_k, cache_sb_v, cache_band_k,
           cache_band_v, cache_fox_k, cache_fox_v, cache_fox_logf, norm_gain, ffn_w_gate, ffn_w_up,
           ffn_w_down, ab_w_in, mla_lat_gain, mla_w_uq, mla_w_ukv, mla_qk_gain, ab_w_out, cd_w_in,
           fox_f_bias, cd_qk_gain, band_rel_bias, cd_w_out):
    depth = norm_gain.shape[0]
    past_len = cache_fox_k.shape[2]
    band_len = cache_band_k.shape[2]
    t_step = x_sample.shape[1]
    assert t_step == CHUNK and past_len % CHUNK == 0 and band_len == BAND_LEFT * CHUNK
    assert x_prompt.shape[1] % CHUNK == 0

    layers = []
    for l in range(depth):
        i = l // 2
        if l % 2 == 0:
            p = _prep_ab(ab_w_in[i], mla_lat_gain[i], mla_w_uq[i], mla_w_ukv[i], mla_qk_gain[i], ab_w_out[i])
        else:
            p = _prep_cd(cd_w_in[i], fox_f_bias[i], cd_qk_gain[i], cd_w_out[i])
            p["rel_bias"] = _rel_bias(band_rel_bias[i])
        g = norm_gain[l]
        p["g_ffn1"], p["g_mix"], p["g_ffn2"] = g[0:1], g[1:2], g[2:3]
        layers.append(p)
    ffn_w = tuple(_to_bf16(w) for w in (ffn_w_gate, ffn_w_up, ffn_w_down))

    caches = (cache_mla_ckv, cache_mla_krope, cache_sb_k, cache_sb_v,
              cache_band_k, cache_band_v, cache_fox_k, cache_fox_v, cache_fox_logf)
    y_p, st_p = _trunk(x_prompt, 0, None, band_len, layers, ffn_w)
    y_s, st_s = _trunk(x_sample, past_len, caches, band_len, layers, ffn_w)
    out = [y_p, y_s]
    for a, c in zip(st_p, st_s):
        out += [a, c]
    return tuple(out)
```

```python
import functools

import jax
import jax.numpy as jnp
from jax import lax
from jax.experimental import pallas as pl
from jax.experimental.pallas import tpu as pltpu

F32 = jnp.float32
BF16 = jnp.bfloat16

EPS = 1e-6
NEG_INF = -1e30
CHUNK = 64
BAND_LEFT = 8
REL_CLIP = 128
N_HEADS = 8
HEAD_DIM = 128
Q_RANK = 512
KV_RANK = 256
ROPE_DIM = 64
NOPE_DIM = 128
QK_HEAD = NOPE_DIM + ROPE_DIM
MLA_QK_PAD = 256
ROPE_THETA = 10000.0
MLA_SCALE = QK_HEAD ** -0.5
ATT_SCALE = HEAD_DIM ** -0.5
LOG2E = 1.4426950408889634
HEAD_PAIR = 2
SB_HEADS = 4
STEP_SB_GROUP = 2
HW = N_HEADS * HEAD_DIM
AB_COLS = 4096
CD_COLS = 6400
BAND_KEYS = (BAND_LEFT + 1) * CHUNK

VMEM_LIMIT_BYTES = 48 * 1024 * 1024
FFN_VMEM_LIMIT_BYTES = 56 * 1024 * 1024


def _params(*sem):
    return pltpu.CompilerParams(dimension_semantics=sem, vmem_limit_bytes=VMEM_LIMIT_BYTES)


def _rms(x, gain, inv_n=None):
    if inv_n is None:
        ms = jnp.mean(x * x, axis=-1, keepdims=True)
    else:
        ms = jnp.sum(x * x, axis=-1, keepdims=True) * inv_n
    return x * lax.rsqrt(ms + EPS) * gain


def _dot(a, b):
    return jnp.dot(a, b, preferred_element_type=F32)


def _lane_tile(x, n):
    return x if n == 1 else jnp.concatenate([x] * n, axis=1)


def _dot_t(a, b):
    return lax.dot_general(a, b, (((1,), (1,)), ((), ())), preferred_element_type=F32)


def _ffn_kernel(x_ref, g_ref, wg_ref, wu_ref, wd_ref, o_ref, h_ref, *, n_f):
    j = pl.program_id(1)

    @pl.when(j == 0)
    def _():
        h_ref[...] = _rms(x_ref[...], g_ref[...]).astype(BF16)
        o_ref[...] = jnp.zeros_like(o_ref)

    h = h_ref[...]
    g = _dot(h, wg_ref[...])
    u = _dot(h, wu_ref[...])
    a = (g / (1.0 + jnp.exp(-g)) * u).astype(BF16)
    o_ref[...] += _dot(a, wd_ref[...])

    @pl.when(j == n_f - 1)
    def _():
        o_ref[...] = x_ref[...] + 0.5 * o_ref[...]


def _ffn(x, gain, wg, wu, wd, l, s):
    n, d = x.shape
    f = wg.shape[-1]
    tm = min(1024, n)
    tf = 512 if f % 512 == 0 else f
    n_f = f // tf
    return pl.pallas_call(
        functools.partial(_ffn_kernel, n_f=n_f),
        grid=(n // tm, n_f),
        in_specs=[
            pl.BlockSpec((tm, d), lambda i, j: (i, 0)),
            pl.BlockSpec((1, d), lambda i, j: (0, 0)),
            pl.BlockSpec((None, None, d, tf), lambda i, j: (l, s, 0, j)),
            pl.BlockSpec((None, None, d, tf), lambda i, j: (l, s, 0, j)),
            pl.BlockSpec((None, None, tf, d), lambda i, j: (l, s, j, 0)),
        ],
        out_specs=pl.BlockSpec((tm, d), lambda i, j: (i, 0)),
        out_shape=jax.ShapeDtypeStruct((n, d), F32),
        scratch_shapes=[pltpu.VMEM((tm, d), BF16)],
        compiler_params=pltpu.CompilerParams(dimension_semantics=("parallel", "arbitrary"),
                                             vmem_limit_bytes=FFN_VMEM_LIMIT_BYTES),
        name="ffn",
    )(x, gain, wg, wu, wd)


def _cast_kernel(x_ref, o_ref):
    o_ref[...] = x_ref[...].astype(BF16)


def _to_bf16(w):
    shape = w.shape
    w2 = w.reshape(-1, shape[-1])
    r, c = w2.shape
    tr = 256 if r % 256 == 0 else r
    out = pl.pallas_call(
        _cast_kernel,
        grid=(r // tr,),
        in_specs=[pl.BlockSpec((tr, c), lambda i: (i, 0))],
        out_specs=pl.BlockSpec((tr, c), lambda i: (i, 0)),
        out_shape=jax.ShapeDtypeStruct((r, c), BF16),
        compiler_params=_params("parallel"),
        name="to_bf16",
    )(w2)
    return out.reshape(shape)


def _proj_in_kernel(x_ref, g_ref, w_ref, o_ref, h_ref):
    @pl.when(pl.program_id(1) == 0)
    def _():
        h_ref[...] = _rms(x_ref[...], g_ref[...]).astype(BF16)

    o_ref[...] = _dot(h_ref[...], w_ref[...])


def _proj_in(x, gain, w, tn):
    n, d = x.shape
    c = w.shape[1]
    tm = min(1024, n)
    return pl.pallas_call(
        _proj_in_kernel,
        grid=(n // tm, c // tn),
        in_specs=[
            pl.BlockSpec((tm, d), lambda i, j: (i, 0)),
            pl.BlockSpec((1, d), lambda i, j: (0, 0)),
            pl.BlockSpec((d, tn), lambda i, j: (0, j)),
        ],
        out_specs=pl.BlockSpec((tm, tn), lambda i, j: (i, j)),
        out_shape=jax.ShapeDtypeStruct((n, c), F32),
        scratch_shapes=[pltpu.VMEM((tm, d), BF16)],
        compiler_params=_params("parallel", "arbitrary"),
        name="proj_in",
    )(x, gain, w)


def _proj_out_kernel(a_ref, b_ref, wa_ref, wb_ref, x_ref, o_ref):
    o_ref[...] = x_ref[...] + _dot(a_ref[...], wa_ref[...]) + _dot(b_ref[...], wb_ref[...])


def _proj_out(oa, ob, wa, wb, x):
    n, d = x.shape
    k = oa.shape[1]
    tm = min(1024, n)
    tn = min(1024, d)
    return pl.pallas_call(
        _proj_out_kernel,
        grid=(n // tm, d // tn),
        in_specs=[
            pl.BlockSpec((tm, k), lambda i, j: (i, 0)),
            pl.BlockSpec((tm, k), lambda i, j: (i, 0)),
            pl.BlockSpec((k, tn), lambda i, j: (0, j)),
            pl.BlockSpec((k, tn), lambda i, j: (0, j)),
            pl.BlockSpec((tm, tn), lambda i, j: (i, j)),
        ],
        out_specs=pl.BlockSpec((tm, tn), lambda i, j: (i, j)),
        out_shape=jax.ShapeDtypeStruct((n, d), F32),
        compiler_params=_params("parallel", "parallel"),
        name="proj_out",
    )(oa, ob, wa, wb, x)


def _rope_slab(x, cs):
    c = cs[:, 0:128]
    s1 = cs[:, 128:256]
    s2 = cs[:, 256:384]
    return x * c + pltpu.roll(x, 96, 1) * s1 + pltpu.roll(x, 32, 1) * s2


def _in_place(first_input, out_ids, targets):
    arrays, aliases = [], {}
    for t, o in zip(targets, out_ids):
        if not isinstance(t, jax.ShapeDtypeStruct):
            aliases[first_input + len(arrays)] = o
            arrays.append(t)
    return arrays, [pl.BlockSpec(memory_space=pl.ANY)] * len(arrays), aliases


def _store_heads(dst_ref, x):
    m = x.shape[0]
    for h in range(N_HEADS):
        dst_ref[pl.ds(h, m, stride=N_HEADS), :] = x[:, h * HEAD_DIM:(h + 1) * HEAD_DIM]


def _ab_post_kernel(sq_ref, sk_ref, sv_ref, tail_ref, cs_ref, gn_ref, wuq_ref, *rest):
    q_ref, ckv_ref, kr_ref, sqb_ref, skb_ref, svb_ref, skf_ref, svf_ref = rest[-8:]
    sqb_ref[...] = (sq_ref[...] * (ATT_SCALE * LOG2E)).astype(BF16)
    sk = sk_ref[...]
    sv = sv_ref[...]
    _store_heads(skf_ref, sk)
    _store_heads(svf_ref, sv)
    skb_ref[...] = sk.astype(BF16)
    svb_ref[...] = sv.astype(BF16)

    cs = cs_ref[...]
    g_cq = gn_ref[0:1, :]
    g_ckv = gn_ref[1:2, 0:KV_RANK]
    gq_n = gn_ref[2:3, 0:128]
    gq_r = gn_ref[3:4, 0:128]
    gk_r = gn_ref[4:5, 0:128]

    cq = _rms(tail_ref[:, 0:Q_RANK], g_cq)
    ckv_ref[...] = _rms(tail_ref[:, Q_RANK:Q_RANK + KV_RANK], g_ckv)
    kr = _rms(tail_ref[:, 768:896], gk_r, 1.0 / ROPE_DIM)
    kr_ref[...] = _rope_slab(kr, cs)[:, 0:ROPE_DIM]

    q = _dot(cq.astype(BF16), wuq_ref[...])
    for h in range(N_HEADS):
        lo = h * MLA_QK_PAD
        qn = _rms(q[:, lo:lo + 128], gq_n)
        qr = _rope_slab(_rms(q[:, lo + 128:lo + 256], gq_r, 1.0 / ROPE_DIM), cs)
        q_ref[:, lo:lo + 128] = (qn * (MLA_SCALE * LOG2E)).astype(BF16)
        q_ref[:, lo + 128:lo + 256] = (qr * (MLA_SCALE * LOG2E)).astype(BF16)


def _ab_post(proj, cs_tab, gains, wuq, layer, stacks):
    n = proj.shape[0]
    tm = min(256, n, cs_tab.shape[0])
    n_tab = cs_tab.shape[0] // tm
    row = lambda i: (i, 0)
    slot = lambda i: (layer, i, 0)
    b16 = lambda w: jax.ShapeDtypeStruct((n, w), BF16)
    same = lambda a: jax.ShapeDtypeStruct(a.shape, a.dtype)
    arrays, any_specs, aliases = _in_place(7, (1, 2, 6, 7), stacks)
    return pl.pallas_call(
        _ab_post_kernel,
        grid=(n // tm,),
        in_specs=[
            pl.BlockSpec((tm, HW), lambda i: (i, 0)),
            pl.BlockSpec((tm, HW), lambda i: (i, 1)),
            pl.BlockSpec((tm, HW), lambda i: (i, 2)),
            pl.BlockSpec((tm, HW), lambda i: (i, 3)),
            pl.BlockSpec((tm, 384), lambda i: (i % n_tab, 0)),
            pl.BlockSpec(gains.shape, lambda i: (0, 0)),
            pl.BlockSpec(wuq.shape, lambda i: (0, 0)),
        ] + any_specs,
        out_specs=[
            pl.BlockSpec((tm, N_HEADS * MLA_QK_PAD), row),
            pl.BlockSpec((None, tm, KV_RANK), slot),
            pl.BlockSpec((None, tm, ROPE_DIM), slot),
            pl.BlockSpec((tm, HW), row),
            pl.BlockSpec((tm, HW), row),
            pl.BlockSpec((tm, HW), row),
            pl.BlockSpec((None, tm * N_HEADS, HEAD_DIM), slot),
            pl.BlockSpec((None, tm * N_HEADS, HEAD_DIM), slot),
        ],
        out_shape=[b16(N_HEADS * MLA_QK_PAD), same(stacks[0]), same(stacks[1]), b16(HW), b16(HW), b16(HW),
                   same(stacks[2]), same(stacks[3])],
        input_output_aliases=aliases,
        compiler_params=_params("parallel"),
        name="ab_post",
    )(proj, proj, proj, proj, cs_tab, gains, wuq, *arrays)


def _mla_kv_kernel(ckv_ref, kr_ref, w_ref, g_ref, k_ref, v_ref):
    kv = _dot(ckv_ref[...].astype(BF16), w_ref[...])
    m = kv.shape[0]
    kr = jnp.concatenate([kr_ref[...], jnp.zeros((m, 128 - ROPE_DIM), F32)], axis=1).astype(BF16)
    g = g_ref[...]
    for h in range(N_HEADS):
        lo = h * 256
        k_ref[:, lo:lo + 128] = _rms(kv[:, lo:lo + 128], g).astype(BF16)
        k_ref[:, lo + 128:lo + 256] = kr
        v_ref[:, h * 128:(h + 1) * 128] = kv[:, lo + 128:lo + 256].astype(BF16)


def _mla_kv(ckv, kr, layer, w_ukv, gk_n):
    m = ckv.shape[1]
    tm = min(512, m)
    row = lambda i: (i, 0)
    return pl.pallas_call(
        _mla_kv_kernel,
        grid=(m // tm,),
        in_specs=[
            pl.BlockSpec((None, tm, KV_RANK), lambda i: (layer, i, 0)),
            pl.BlockSpec((None, tm, ROPE_DIM), lambda i: (layer, i, 0)),
            pl.BlockSpec(w_ukv.shape, lambda i: (0, 0)),
            pl.BlockSpec((1, 128), lambda i: (0, 0)),
        ],
        out_specs=[
            pl.BlockSpec((tm, N_HEADS * MLA_QK_PAD), row),
            pl.BlockSpec((tm, HW), row),
        ],
        out_shape=[
            jax.ShapeDtypeStruct((m, N_HEADS * MLA_QK_PAD), BF16),
            jax.ShapeDtypeStruct((m, HW), BF16),
        ],
        compiler_params=_params("parallel"),
        name="mla_kv",
    )(ckv, kr, w_ukv, gk_n)


def _cd_post_kernel(bq_ref, bk_ref, bv_ref, fq_ref, fk_ref, fv_ref, fl_ref, gn_ref, fb_ref, *rest, pad_tiles):
    bqb_ref, bkb_ref, bvb_ref, fqb_ref, fkb_ref, fvb_ref, bkt_ref, bvt_ref, fks_ref, fvs_ref, lf_ref = rest[-11:]
    live = pl.program_id(1) >= pad_tiles
    m = bq_ref.shape[0]
    bv = bv_ref[...]
    _store_heads(bvt_ref, bv)
    bvb_ref[...] = jnp.where(live, bv, 0.0).astype(BF16)
    fv = fv_ref[...]
    _store_heads(fvs_ref, fv)
    fvb_ref[...] = fv.astype(BF16)
    for h in range(N_HEADS):
        sl = slice(h * 128, (h + 1) * 128)
        rows = pl.ds(h, m, stride=N_HEADS)
        bqb_ref[:, sl] = (_rms(bq_ref[:, sl], gn_ref[0:1, :]) * (ATT_SCALE * LOG2E)).astype(BF16)
        bk = _rms(bk_ref[:, sl], gn_ref[1:2, :])
        bkt_ref[rows, :] = bk
        bkb_ref[:, sl] = jnp.where(live, bk, 0.0).astype(BF16)
        fqb_ref[:, sl] = (_rms(fq_ref[:, sl], gn_ref[2:3, :]) * (ATT_SCALE * LOG2E)).astype(BF16)
        fk = _rms(fk_ref[:, sl], gn_ref[3:4, :])
        fks_ref[rows, :] = fk
        fkb_ref[:, sl] = fk.astype(BF16)
    z = fl_ref[:, 0:128] + fb_ref[...]
    lf = -(jnp.maximum(-z, 0.0) + jnp.log1p(jnp.exp(-jnp.abs(z))))
    lf_ref[...] = lf[:, 0:N_HEADS]


def _cd_post(proj, gains, f_bias, b, t, pad_rows, layer, tail_slot, tail_rows, states):
    n = proj.shape[0]
    tm = min(256, t)
    tpb = t // tm
    pad_tiles = pad_rows // tm
    tail_tiles = tail_rows // tm
    tile = lambda s: jnp.maximum(s - pad_tiles, 0)
    src = lambda bi, s: bi * tpb + tile(s)
    tail = lambda bi, s: (tail_slot, bi * tail_tiles + jnp.maximum(tile(s) - (tpb - tail_tiles), 0), 0)
    col = lambda c: pl.BlockSpec((tm, HW), lambda bi, s, c=c: (src(bi, s), c))
    b16o = jax.ShapeDtypeStruct((n, HW), BF16)
    padded = jax.ShapeDtypeStruct((b * (pad_rows + t), HW), BF16)
    blk = pl.BlockSpec((tm, HW), lambda bi, s: (src(bi, s), 0))
    pblk = pl.BlockSpec((tm, HW), lambda bi, s: (bi * (tpb + pad_tiles) + s, 0))
    heads_blk = lambda imap: pl.BlockSpec((None, tm * N_HEADS, HEAD_DIM), imap)
    slot = lambda bi, s: (layer, src(bi, s), 0)
    same = lambda a: jax.ShapeDtypeStruct(a.shape, a.dtype)
    arrays, any_specs, aliases = _in_place(9, (6, 7, 8, 9, 10), states)
    return pl.pallas_call(
        functools.partial(_cd_post_kernel, pad_tiles=pad_tiles),
        grid=(b, tpb + pad_tiles),
        in_specs=[col(0), col(1), col(2), col(3), col(4), col(5),
                  pl.BlockSpec((tm, 256), lambda bi, s: (src(bi, s), 6 * HW // 256)),
                  pl.BlockSpec((4, 128), lambda bi, s: (0, 0)),
                  pl.BlockSpec((1, 128), lambda bi, s: (0, 0))] + any_specs,
        out_specs=[blk, pblk, pblk, blk, blk, blk,
                   heads_blk(tail), heads_blk(tail), heads_blk(slot), heads_blk(slot),
                   pl.BlockSpec((None, tm, N_HEADS), slot)],
        out_shape=[b16o, padded, padded, b16o, b16o, b16o] + [same(a) for a in states],
        input_output_aliases=aliases,
        compiler_params=_params("arbitrary", "arbitrary"),
        name="cd_post",
    )(proj, proj, proj, proj, proj, proj, proj, gains, f_bias, *arrays)


def _band_buf_kernel(cache_ref, new_ref, *rest):
    o_ref = rest[-1]
    keep = cache_ref.shape[0] - new_ref.shape[0]
    o_ref[0:keep, :] = cache_ref[new_ref.shape[0]:, :]
    o_ref[keep:, :] = new_ref[...]


def _band_buf(cache, new, layer, out):
    _, b, rows, _ = cache.shape
    blk = pl.BlockSpec((None, None, rows, HEAD_DIM), lambda i: (layer, i, 0, 0))
    arrays, any_specs, aliases = _in_place(2, (0,), (out,))
    return pl.pallas_call(
        _band_buf_kernel,
        grid=(b,),
        in_specs=[blk, pl.BlockSpec((None, new.shape[1], HEAD_DIM), lambda i: (i, 0, 0))] + any_specs,
        out_specs=blk,
        out_shape=jax.ShapeDtypeStruct(out.shape, out.dtype),
        input_output_aliases=aliases,
        compiler_params=_params("parallel"),
        name="band_buf",
    )(cache, new, *arrays)


def _cumsum_kernel(x_ref, o_ref, *, n_blk, scale):
    jj = lax.broadcasted_iota(jnp.int32, (128, 128), 0)
    ss = lax.broadcasted_iota(jnp.int32, (128, 128), 1)
    upper = jnp.where(jj <= ss, 1.0, 0.0).astype(BF16)

    def body(i, carry):
        x = x_ref[i]
        hi = x.astype(BF16)
        r1 = x - hi.astype(F32)
        mid = r1.astype(BF16)
        lo = (r1 - mid.astype(F32)).astype(BF16)
        c = _dot(hi, upper) + _dot(mid, upper) + _dot(lo, upper) + carry
        o_ref[i] = c * scale
        return c[:, 127:128]

    lax.fori_loop(0, n_blk, body, jnp.zeros((N_HEADS, 1), F32))


def _cumsum_rows(lf, scale):
    b, length, nh = lf.shape
    n_blk = -(-length // 128)
    x = jnp.pad(jnp.swapaxes(lf, 1, 2), ((0, 0), (0, 0), (0, n_blk * 128 - length)))
    x = jnp.swapaxes(x.reshape(b, nh, n_blk, 128), 1, 2)
    out = pl.pallas_call(
        functools.partial(_cumsum_kernel, n_blk=n_blk, scale=scale),
        grid=(b,),
        in_specs=[pl.BlockSpec((None, n_blk, nh, 128), lambda i: (i, 0, 0, 0))],
        out_specs=pl.BlockSpec((None, n_blk, nh, 128), lambda i: (i, 0, 0, 0)),
        out_shape=jax.ShapeDtypeStruct((b, n_blk, nh, 128), F32),
        compiler_params=_params("parallel"),
        name="cumsum",
    )(x)
    return jnp.swapaxes(out, 1, 2).reshape(b, nh, n_blk * 128)[:, :, :length]


def _rel_bias_kernel(tab_ref, o_ref, *, n_tab):
    h = pl.program_id(0)
    i = lax.broadcasted_iota(jnp.int32, (CHUNK, BAND_KEYS), 0)
    j = lax.broadcasted_iota(jnp.int32, (CHUNK, BAND_KEYS), 1)
    idx = jnp.clip(BAND_LEFT * CHUNK + i - j, -REL_CLIP, REL_CLIP) + REL_CLIP

    def body(r, acc):
        return jnp.where(idx == r, tab_ref[h, r] * LOG2E, acc)

    o_ref[...] = lax.fori_loop(0, n_tab, body, jnp.zeros((CHUNK, BAND_KEYS), F32))


def _rel_bias(table):
    nh, n_tab = table.shape
    return pl.pallas_call(
        functools.partial(_rel_bias_kernel, n_tab=n_tab),
        grid=(nh,),
        in_specs=[pl.BlockSpec(memory_space=pltpu.SMEM)],
        out_specs=pl.BlockSpec((None, CHUNK, BAND_KEYS), lambda h: (h, 0, 0)),
        out_shape=jax.ShapeDtypeStruct((nh, CHUNK, BAND_KEYS), F32),
        compiler_params=_params("arbitrary"),
        name="rel_bias",
    )(table)


def _flash_kernel(*refs, tq, mode, has_bias):
    if has_bias:
        q_ref, k_ref, v_ref, cq_ref, ck_ref, o_ref, m_scr, acc_scr = refs
    else:
        q_ref, k_ref, v_ref, o_ref, m_scr, acc_scr = refs
    hg = pl.program_id(1)
    qi = pl.program_id(2)
    dk = q_ref.shape[1] // HEAD_PAIR
    reps = tq // HEAD_DIM
    m_scr[...] = jnp.full(m_scr.shape, NEG_INF, F32)
    acc_scr[...] = jnp.zeros_like(acc_scr)
    ones = jnp.ones((tq, HEAD_DIM), BF16)
    if has_bias:
        lane = lax.broadcasted_iota(jnp.int32, cq_ref.shape, 1)
        cqs = [jnp.broadcast_to(
            jnp.sum(jnp.where(lane == hg * HEAD_PAIR + j, cq_ref[...], 0.0), axis=-1, keepdims=True),
            (tq, HEAD_DIM)) for j in range(HEAD_PAIR)]

    def step(kbs, diag_last):
        chains = [(kb, j) for kb in kbs for j in range(HEAD_PAIR)]
        scores = [_dot_t(q_ref[:, j * dk:(j + 1) * dk],
                         k_ref[pl.ds(pl.multiple_of(kb * tq, tq), tq), j * dk:(j + 1) * dk]) for kb, j in chains]
        for (kb, j), s in zip(chains, scores):
            ks = pl.multiple_of(kb * tq, tq)
            masked = diag_last and kb is kbs[-1]
            if has_bias:
                s = s - ck_ref[j, pl.ds(kb, 1), :]
            if masked:
                row = lax.broadcasted_iota(jnp.int32, (tq, tq), 0)
                col = lax.broadcasted_iota(jnp.int32, (tq, tq), 1)
                if mode == "chunk":
                    ok = lax.shift_right_logical(col, 6) <= lax.shift_right_logical(row, 6)
                else:
                    ok = col <= row
                s = jnp.where(ok, s, NEG_INF)
            m_old = m_scr[j]
            rmax = jnp.max(s, axis=-1, keepdims=True)
            if has_bias:
                m_new = jnp.maximum(m_old, rmax + cqs[j])
                shift = m_new - cqs[j]
            else:
                m_new = jnp.maximum(m_old, rmax)
                shift = m_new
            p = jnp.exp2(s - _lane_tile(shift, reps))
            alpha = jnp.exp2(m_old - m_new)
            v1 = jnp.concatenate([v_ref[pl.ds(ks, tq), j * HEAD_DIM:(j + 1) * HEAD_DIM], ones], axis=1)
            acc_scr[j] = _lane_tile(alpha, 2) * acc_scr[j] + _dot(p.astype(BF16), v1)
            m_scr[j] = m_new

    def body(i, c):
        step([2 * i, 2 * i + 1], False)
        return c

    lax.fori_loop(0, lax.shift_right_logical(qi, 1), body, 0)

    @pl.when(lax.bitwise_and(qi, 1) == 1)
    def _():
        step([qi - 1, qi], True)

    @pl.when(lax.bitwise_and(qi, 1) == 0)
    def _():
        step([qi], True)

    for j in range(HEAD_PAIR):
        acc = acc_scr[j]
        o_ref[:, j * HEAD_DIM:(j + 1) * HEAD_DIM] = (acc[:, :HEAD_DIM] / acc[:, HEAD_DIM:]).astype(BF16)


def _flash(q, k, v, mode, cq=None, ck=None):
    b, t, _ = q.shape
    dk = q.shape[2] // N_HEADS
    tq = min(512, t)
    nq = t // tq
    has_bias = cq is not None
    hp = HEAD_PAIR
    in_specs = [
        pl.BlockSpec((None, tq, hp * dk), lambda bi, h, qi: (bi, qi, h)),
        pl.BlockSpec((None, t, hp * dk), lambda bi, h, qi: (bi, 0, h)),
        pl.BlockSpec((None, t, hp * HEAD_DIM), lambda bi, h, qi: (bi, 0, h)),
    ]
    args = [q, k, v]
    if has_bias:
        in_specs += [
            pl.BlockSpec((None, tq, N_HEADS), lambda bi, h, qi: (bi, qi, 0)),
            pl.BlockSpec((None, hp, nq, tq), lambda bi, h, qi: (bi, h, 0, 0)),
        ]
        args += [cq, ck.reshape(b, N_HEADS, nq, tq)]
    return pl.pallas_call(
        functools.partial(_flash_kernel, tq=tq, mode=mode, has_bias=has_bias),
        grid=(b, N_HEADS // hp, nq),
        in_specs=in_specs,
        out_specs=pl.BlockSpec((None, tq, hp * HEAD_DIM), lambda bi, h, qi: (bi, qi, h)),
        out_shape=jax.ShapeDtypeStruct((b, t, HW), BF16),
        scratch_shapes=[pltpu.VMEM((hp, tq, HEAD_DIM), F32), pltpu.VMEM((hp, tq, 2 * HEAD_DIM), F32)],
        compiler_params=_params("parallel", "parallel", "arbitrary"),
        name="flash_" + mode,
    )(*args)


def _sb_stage1(z, tri, ok):
    sp = jnp.maximum(z, 0.0) + jnp.log2(1.0 + jnp.exp2(-jnp.abs(z)))
    lsig = z - sp
    if ok is not None:
        sp = jnp.where(ok, sp, 0.0)
    right = _dot(sp.astype(BF16), tri)
    return lsig, right, jnp.sum(sp, axis=-1, keepdims=True)


def _sb_stage2(lsig, right, c, v, ok):
    n = lsig.shape[1]
    c_wide = _lane_tile(c, n // HEAD_DIM) if n % HEAD_DIM == 0 else c[:, :n]
    w = jnp.exp2(lsig - right + c_wide)
    if ok is not None:
        w = jnp.where(ok, w, 0.0)
    return _dot(w.astype(BF16), v)


def _tri(n):
    jj = lax.broadcasted_iota(jnp.int32, (n, n), 0)
    ss = lax.broadcasted_iota(jnp.int32, (n, n), 1)
    return jnp.where(jj > ss, 1.0, 0.0).astype(BF16)


def _sb_kernel(q_ref, k_ref, v_ref, o_ref, c_scr, acc_scr, *, tq, sub):
    qi = pl.program_id(2)
    n_sub = tq // sub
    tri = _tri(sub)
    c_scr[...] = jnp.zeros_like(c_scr)
    acc_scr[...] = jnp.zeros_like(acc_scr)

    def block(kb, diag):
        chains = [(d, j) for d in reversed(range(n_sub)) for j in range(SB_HEADS)]
        hs = lambda j: slice(j * HEAD_DIM, (j + 1) * HEAD_DIM)
        ks = lambda d: pl.multiple_of(kb * tq + d * sub, sub)
        oks = {}
        for d in range(n_sub):
            oks[d] = None
            if diag:
                row = lax.broadcasted_iota(jnp.int32, (tq, sub), 0)
                col = lax.broadcasted_iota(jnp.int32, (tq, sub), 1)
                oks[d] = col + d * sub < row
        zs = {(d, j): _dot_t(q_ref[:, hs(j)], k_ref[pl.ds(ks(d), sub), hs(j)]) for d, j in chains}
        mids = {dj: _sb_stage1(zs[dj], tri, oks[dj[0]]) for dj in chains}
        for d, j in chains:
            lsig, right, rowsum = mids[(d, j)]
            c = c_scr[j]
            acc_scr[j] += _sb_stage2(lsig, right, c, v_ref[pl.ds(ks(d), sub), hs(j)], oks[d])
            c_scr[j] = c - rowsum

    block(qi, True)

    def body(i, c):
        block(qi - 1 - i, False)
        return c

    lax.fori_loop(0, qi, body, 0)
    for j in range(SB_HEADS):
        o_ref[:, j * HEAD_DIM:(j + 1) * HEAD_DIM] = acc_scr[j].astype(BF16)


def _sb_prompt(q, k, v):
    b, t, _ = q.shape
    tq = min(512, t)
    sub = min(256, tq)
    hp = SB_HEADS
    return pl.pallas_call(
        functools.partial(_sb_kernel, tq=tq, sub=sub),
        grid=(b, N_HEADS // hp, t // tq),
        in_specs=[
            pl.BlockSpec((None, tq, hp * HEAD_DIM), lambda bi, h, qi: (bi, qi, h)),
            pl.BlockSpec((None, t, hp * HEAD_DIM), lambda bi, h, qi: (bi, 0, h)),
            pl.BlockSpec((None, t, hp * HEAD_DIM), lambda bi, h, qi: (bi, 0, h)),
        ],
        out_specs=pl.BlockSpec((None, tq, hp * HEAD_DIM), lambda bi, h, qi: (bi, qi, h)),
        out_shape=jax.ShapeDtypeStruct((b, t, HW), BF16),
        scratch_shapes=[pltpu.VMEM((hp, tq, HEAD_DIM), F32), pltpu.VMEM((hp, tq, HEAD_DIM), F32)],
        compiler_params=_params("parallel", "parallel", "arbitrary"),
        name="sb_prompt",
    )(q, k, v)


def _band_kernel(q_ref, k_ref, v_ref, b_ref, o_ref, *, n_chunks, unroll):
    col = lax.broadcasted_iota(jnp.int32, (CHUNK, BAND_KEYS), 1)
    ones = jnp.ones((BAND_KEYS, HEAD_DIM), BF16)

    def body(i, carry):
        chains = [(u, j) for u in range(unroll) for j in range(HEAD_PAIR)]
        hs = lambda j: slice(j * HEAD_DIM, (j + 1) * HEAD_DIM)
        qs = lambda u: pl.multiple_of((i * unroll + u) * CHUNK, CHUNK)
        scores = {(u, j): _dot_t(q_ref[pl.ds(qs(u), CHUNK), hs(j)], k_ref[pl.ds(qs(u), BAND_KEYS), hs(j)])
                  for u, j in chains}
        for u, j in chains:
            ok = col >= (BAND_LEFT - (i * unroll + u)) * CHUNK
            s = jnp.where(ok, scores[(u, j)] + b_ref[j], NEG_INF)
            p = jnp.exp2(s - jnp.max(s, axis=-1, keepdims=True))
            r = _dot(p.astype(BF16), jnp.concatenate([v_ref[pl.ds(qs(u), BAND_KEYS), hs(j)], ones], axis=1))
            o_ref[pl.ds(qs(u), CHUNK), hs(j)] = (r[:, :HEAD_DIM] / r[:, HEAD_DIM:]).astype(BF16)
        return carry

    lax.fori_loop(0, n_chunks // unroll, body, 0)


def _band_prompt(q, k, v, bias):
    b, t, _ = q.shape
    tp = k.shape[1]
    hp = HEAD_PAIR
    n_chunks = t // CHUNK
    unroll = 4 if n_chunks % 4 == 0 else 1
    return pl.pallas_call(
        functools.partial(_band_kernel, n_chunks=n_chunks, unroll=unroll),
        grid=(b, N_HEADS // hp),
        in_specs=[
            pl.BlockSpec((None, t, hp * HEAD_DIM), lambda bi, h: (bi, 0, h)),
            pl.BlockSpec((None, tp, hp * HEAD_DIM), lambda bi, h: (bi, 0, h)),
            pl.BlockSpec((None, tp, hp * HEAD_DIM), lambda bi, h: (bi, 0, h)),
            pl.BlockSpec((hp, CHUNK, BAND_KEYS), lambda bi, h: (h, 0, 0)),
        ],
        out_specs=pl.BlockSpec((None, t, hp * HEAD_DIM), lambda bi, h: (bi, 0, h)),
        out_shape=jax.ShapeDtypeStruct((b, t, HW), BF16),
        compiler_params=_params("parallel", "parallel"),
        name="band_prompt",
    )(q, k, v, bias)


def _step_softmax_kernel(*refs, mode):
    if mode == "fox":
        q_ref, kp_ref, vp_ref, kn_ref, vn_ref, cq_ref, ckp_ref, ckn_ref, o_ref = refs
    elif mode == "band":
        q_ref, kp_ref, vp_ref, kn_ref, vn_ref, bp_ref, bn_ref, o_ref = refs
    else:
        q_ref, kp_ref, vp_ref, kn_ref, vn_ref, o_ref = refs
    t = q_ref.shape[0]
    dk = q_ref.shape[1] // N_HEADS
    if mode == "mla":
        kp_h = lambda h: kp_ref[:, h * dk:(h + 1) * dk]
        vp_h = lambda h: vp_ref[:, h * HEAD_DIM:(h + 1) * HEAD_DIM]
    else:
        p_rows = kp_ref.shape[0] // N_HEADS
        kp_h = lambda h: kp_ref[pl.ds(h, p_rows, stride=N_HEADS), :].astype(BF16)
        vp_h = lambda h: vp_ref[pl.ds(h, p_rows, stride=N_HEADS), :].astype(BF16)
    scores = []
    for h in range(N_HEADS):
        ks = slice(h * dk, (h + 1) * dk)
        scores.append((_dot_t(q_ref[:, ks], kp_h(h)), _dot_t(q_ref[:, ks], kn_ref[:, ks])))
    for h in range(N_HEADS):
        vs = slice(h * HEAD_DIM, (h + 1) * HEAD_DIM)
        sp, sn = scores[h]
        if mode == "fox":
            cq = cq_ref[:, h:h + 1]
            sp = sp + (cq - ckp_ref[h:h + 1, :])
            sn = sn + (cq - ckn_ref[h:h + 1, :])
            row = lax.broadcasted_iota(jnp.int32, (t, t), 0)
            col = lax.broadcasted_iota(jnp.int32, (t, t), 1)
            sn = jnp.where(col <= row, sn, NEG_INF)
        elif mode == "band":
            sp = sp + bp_ref[h]
            sn = sn + bn_ref[h]
        m = jnp.maximum(jnp.max(sp, axis=-1, keepdims=True), jnp.max(sn, axis=-1, keepdims=True))
        pp = jnp.exp2(sp - m)
        pn = jnp.exp2(sn - m)
        l = jnp.sum(pp, axis=-1, keepdims=True) + jnp.sum(pn, axis=-1, keepdims=True)
        o = _dot(pp.astype(BF16), vp_h(h)) + _dot(pn.astype(BF16), vn_ref[:, vs])
        o_ref[:, vs] = (o / l).astype(BF16)


def _cache_spec(cache, layer):
    return pl.BlockSpec((None, None) + cache.shape[2:], lambda i: (layer, i, 0, 0))


def _step_softmax(mode, layer, q, kp, vp, kn, vn, *extra):
    b, t, qw = q.shape
    full = lambda a: pl.BlockSpec((None,) + a.shape[1:], lambda i: (i,) + (0,) * (a.ndim - 1))
    cached = full if mode == "mla" else (lambda a: _cache_spec(a, layer))
    in_specs = [full(q), cached(kp), cached(vp), full(kn), full(vn)]
    if mode == "fox":
        in_specs += [full(e) for e in extra]
    elif mode == "band":
        in_specs += [pl.BlockSpec(e.shape, lambda i: (0, 0, 0)) for e in extra]
    return pl.pallas_call(
        functools.partial(_step_softmax_kernel, mode=mode),
        grid=(b,),
        in_specs=in_specs,
        out_specs=pl.BlockSpec((None, t, HW), lambda i: (i, 0, 0)),
        out_shape=jax.ShapeDtypeStruct((b, t, HW), BF16),
        compiler_params=_params("parallel"),
        name="step_" + mode,
    )(q, kp, vp, kn, vn, *extra)


def _step_sb_kernel(q_ref, kp_ref, vp_ref, kn_ref, vn_ref, o_ref, *, blk):
    t = q_ref.shape[0]
    p = kp_ref.shape[0] // N_HEADS
    tri_n = _tri(t)
    tri_p = _tri(blk)
    row = lax.broadcasted_iota(jnp.int32, (t, t), 0)
    col = lax.broadcasted_iota(jnp.int32, (t, t), 1)
    blocks = list(reversed(range(p // blk)))
    hsl = lambda h: slice(h * HEAD_DIM, (h + 1) * HEAD_DIM)
    rows = lambda kb, h: pl.ds(kb * blk * N_HEADS + h, blk, stride=N_HEADS)
    for h0 in range(0, N_HEADS, STEP_SB_GROUP):
        heads = range(h0, h0 + STEP_SB_GROUP)
        mids = {}
        for h in heads:
            mids[(h, None)] = _sb_stage1(_dot_t(q_ref[:, hsl(h)], kn_ref[:, hsl(h)]), tri_n, col < row)
            for kb in blocks:
                mids[(h, kb)] = _sb_stage1(_dot_t(q_ref[:, hsl(h)], kp_ref[rows(kb, h), :].astype(BF16)), tri_p, None)
        for h in heads:
            lsig, right, rowsum = mids[(h, None)]
            c = jnp.zeros((t, HEAD_DIM), F32)
            acc = _sb_stage2(lsig, right, c, vn_ref[:, hsl(h)], col < row)
            c = c - rowsum
            for kb in blocks:
                lsig, right, rowsum = mids[(h, kb)]
                acc = acc + _sb_stage2(lsig, right, c, vp_ref[rows(kb, h), :].astype(BF16), None)
                c = c - rowsum
            o_ref[:, hsl(h)] = acc.astype(BF16)


def _step_sb(layer, q, kp, vp, kn, vn):
    b, t, _ = q.shape
    p = kp.shape[2] // N_HEADS
    blk = min(256, p)
    full = lambda a: pl.BlockSpec((None,) + a.shape[1:], lambda i: (i, 0, 0))
    return pl.pallas_call(
        functools.partial(_step_sb_kernel, blk=blk),
        grid=(b,),
        in_specs=[full(q), _cache_spec(kp, layer), _cache_spec(vp, layer), full(kn), full(vn)],
        out_specs=pl.BlockSpec((None, t, HW), lambda i: (i, 0, 0)),
        out_shape=jax.ShapeDtypeStruct((b, t, HW), BF16),
        compiler_params=_params("parallel"),
        name="step_sb",
    )(q, kp, vp, kn, vn)


def _rope_tables(pos, rows):
    half = ROPE_DIM // 2
    inv = ROPE_THETA ** (-jnp.arange(half, dtype=F32) / half)
    ang = pos.astype(F32)[:, None] * inv[None, :]
    c, s, z = jnp.cos(ang), jnp.sin(ang), jnp.zeros_like(ang)
    tab = jnp.concatenate([c, c, z, z, -s, z, z, z, z, s, z, z], axis=1)
    return jnp.tile(tab, (max(1, rows // tab.shape[0]), 1))


def _prep_ab(w_in, lat_gain, w_uq, w_ukv, qk_gain, w_out):
    d = w_in.shape[0]
    cq, ckv, kr, sq, sk, sv = jnp.split(w_in, [512, 768, 832, 1856, 2880], axis=1)
    w_in_p = jnp.concatenate([sq, sk, sv, cq, ckv, kr, jnp.zeros((d, AB_COLS - w_in.shape[1]), F32)], axis=1)
    uq = w_uq.reshape(Q_RANK, N_HEADS, QK_HEAD)
    uq = jnp.pad(uq, ((0, 0), (0, 0), (0, MLA_QK_PAD - QK_HEAD))).reshape(Q_RANK, N_HEADS * MLA_QK_PAD)
    gains = jnp.zeros((8, Q_RANK), F32)
    gains = gains.at[0, :].set(lat_gain[:Q_RANK])
    gains = gains.at[1, :KV_RANK].set(lat_gain[Q_RANK:])
    gains = gains.at[2, :NOPE_DIM].set(qk_gain[0, :NOPE_DIM])
    gains = gains.at[3, :ROPE_DIM].set(qk_gain[0, NOPE_DIM:])
    gains = gains.at[4, :ROPE_DIM].set(qk_gain[1, NOPE_DIM:])
    return dict(w_in=w_in_p.astype(BF16), w_uq=uq.astype(BF16), w_ukv=w_ukv.astype(BF16), gains=gains,
                gk_n=qk_gain[1:2, :NOPE_DIM], w_out_a=w_out[:HW].astype(BF16), w_out_b=w_out[HW:].astype(BF16))


def _prep_cd(w_in, f_bias, qk_gain, w_out):
    d = w_in.shape[0]
    w_in_p = jnp.pad(w_in, ((0, 0), (0, CD_COLS - w_in.shape[1])))
    fb = jnp.pad(f_bias, (0, 128 - N_HEADS)).reshape(1, 128)
    return dict(w_in=w_in_p.astype(BF16), gains=qk_gain, f_bias=fb,
                w_out_a=w_out[:HW].astype(BF16), w_out_b=w_out[HW:].astype(BF16))


def _heads_view(c):
    l, b, p, h, d = c.shape
    return c.reshape(l, b, p * h, d)


def _ab_mixer(x, b, t, pos, i, caches, states, p):
    n = x.shape[0]
    proj = _proj_in(x, p["g_mix"], p["w_in"], 1024)
    cs = _rope_tables(pos, min(256, n))
    q, ckv, kr, sq, skb, svb, sk, sv = _ab_post(proj, cs, p["gains"], p["w_uq"], i, states)
    kf, vf = _mla_kv(ckv, kr, i, p["w_ukv"], p["gk_n"])
    r3 = lambda a: a.reshape(b, t, a.shape[-1])
    if caches is None:
        o_mla = _flash(r3(q), r3(kf), r3(vf), "chunk")
        o_sb = _sb_prompt(r3(sq), r3(skb), r3(svb))
    else:
        c_ckv, c_kr, c_sk, c_sv = caches
        na, _, pl_, _ = c_ckv.shape
        kc, vc = _mla_kv(c_ckv.reshape(na, b * pl_, KV_RANK), c_kr.reshape(na, b * pl_, ROPE_DIM), i,
                         p["w_ukv"], p["gk_n"])
        o_mla = _step_softmax("mla", i, r3(q), kc.reshape(b, pl_, -1), vc.reshape(b, pl_, -1), r3(kf), r3(vf))
        o_sb = _step_sb(i, r3(sq), _heads_view(c_sk), _heads_view(c_sv), r3(skb), r3(svb))
    x = _proj_out(o_mla.reshape(n, HW), o_sb.reshape(n, HW), p["w_out_a"], p["w_out_b"], x)
    return x, (ckv, kr, sk, sv)


def _cd_mixer(x, b, t, i, caches, band_len, states, p):
    n = x.shape[0]
    proj = _proj_in(x, p["g_mix"], p["w_in"], 1280)
    r3 = lambda a: a.reshape(b, -1, a.shape[-1])
    if caches is None:
        assert t >= band_len
        outs = _cd_post(proj, p["gains"], p["f_bias"], b, t, BAND_LEFT * CHUNK, i, i, band_len, states)
        bq, bkb, bvb, fq, fkb, fvb, bk_buf, bv_buf, fk, fv, lf = outs
        o_band = _band_prompt(r3(bq), r3(bkb), r3(bvb), p["rel_bias"])
        cum = _cumsum_rows(lf[i].reshape(b, t, N_HEADS), LOG2E)
        o_fox = _flash(r3(fq), r3(fkb), r3(fvb), "causal", jnp.swapaxes(cum, 1, 2), cum)
    else:
        c_bk, c_bv, c_fk, c_fv, c_lf = caches
        pl_ = c_fk.shape[2]
        bl = c_bk.shape[2]
        new_rows = jax.ShapeDtypeStruct((1, n * N_HEADS, HEAD_DIM), F32)
        outs = _cd_post(proj, p["gains"], p["f_bias"], b, t, 0, i, 0, t, (new_rows, new_rows) + states[2:])
        bq, bkb, bvb, fq, fkb, fvb, bk_new, bv_new, fk, fv, lf = outs
        o_band = _step_softmax("band", i, r3(bq), _heads_view(c_bk), _heads_view(c_bv), r3(bkb), r3(bvb),
                               p["rel_bias"][:, :, :bl], p["rel_bias"][:, :, bl:])
        shape4 = (states[0].shape[0], b, band_len * N_HEADS, HEAD_DIM)
        buf4 = lambda a: (jax.ShapeDtypeStruct(shape4, F32) if isinstance(a, jax.ShapeDtypeStruct)
                          else a.reshape(shape4))
        bk_buf = _band_buf(_heads_view(c_bk), bk_new.reshape(b, t * N_HEADS, HEAD_DIM), i, buf4(states[0]))
        bv_buf = _band_buf(_heads_view(c_bv), bv_new.reshape(b, t * N_HEADS, HEAD_DIM), i, buf4(states[1]))
        bk_buf, bv_buf = bk_buf.reshape(states[0].shape), bv_buf.reshape(states[1].shape)
        cum = _cumsum_rows(jnp.concatenate([c_lf[i].astype(F32), lf[i].reshape(b, t, N_HEADS)], 1), LOG2E)
        o_fox = _step_softmax("fox", i, r3(fq), _heads_view(c_fk), _heads_view(c_fv), r3(fkb), r3(fvb),
                              jnp.swapaxes(cum[:, :, pl_:], 1, 2), cum[:, :, :pl_], cum[:, :, pl_:])
    x = _proj_out(o_band.reshape(n, HW), o_fox.reshape(n, HW), p["w_out_a"], p["w_out_b"], x)
    return x, (bk_buf, bv_buf, fk, fv, lf)


def _trunk(x3, past_len, caches, band_len, layers, ffn_w):
    b, t, d = x3.shape
    n = b * t
    x = x3.reshape(n, d)
    pos = jnp.arange(past_len, past_len + t)
    n_ab, n_cd = (len(layers) + 1) // 2, len(layers) // 2
    stack = lambda *shape: jax.ShapeDtypeStruct(shape, F32)
    ab = (stack(n_ab, n, KV_RANK), stack(n_ab, n, ROPE_DIM),
          stack(n_ab, n * N_HEADS, HEAD_DIM), stack(n_ab, n * N_HEADS, HEAD_DIM))
    cd = (stack(n_cd, b * band_len * N_HEADS, HEAD_DIM), stack(n_cd, b * band_len * N_HEADS, HEAD_DIM),
          stack(n_cd, n * N_HEADS, HEAD_DIM), stack(n_cd, n * N_HEADS, HEAD_DIM), stack(n_cd, n, N_HEADS))
    for l, p in enumerate(layers):
        i = l // 2
        x = _ffn(x, p["g_ffn1"], *ffn_w, l, 0)
        if l % 2 == 0:
            x, ab = _ab_mixer(x, b, t, pos, i, None if caches is None else caches[:4], ab, p)
        else:
            x, cd = _cd_mixer(x, b, t, i, None if caches is None else caches[4:], band_len, cd, p)
        x = _ffn(x, p["g_ffn2"], *ffn_w, l, 1)
    h5 = lambda a, rows: a.reshape(a.shape[0], b, rows, N_HEADS, HEAD_DIM)
    states = [ab[0].reshape(n_ab, b, t, KV_RANK), ab[1].reshape(n_ab, b, t, ROPE_DIM), h5(ab[2], t), h5(ab[3], t),
              h5(cd[0], band_len), h5(cd[1], band_len), h5(cd[2], t), h5(cd[3], t), cd[4].reshape(n_cd, b, t, N_HEADS)]
    return x.reshape(b, t, d), states


def kernel(x_prompt, x_sample, cache_mla_ckv, cache_mla_krope, cache_sb_k, cache_sb_v, cache_band_k,
           cache_band_v, cache_fox_k, cache_fox_v, cache_fox_logf, norm_gain, ffn_w_gate, ffn_w_up,
           ffn_w_down, ab_w_in, mla_lat_gain, mla_w_uq, mla_w_ukv, mla_qk_gain, ab_w_out, cd_w_in,
           fox_f_bias, cd_qk_gain, band_rel_bias, cd_w_out):
    depth = norm_gain.shape[0]
    past_len = cache_fox_k.shape[2]
    band_len = cache_band_k.shape[2]
    t_step = x_sample.shape[1]
    assert t_step == CHUNK and past_len % CHUNK == 0 and band_len == BAND_LEFT * CHUNK
    assert x_prompt.shape[1] % CHUNK == 0

    layers = []
    for l in range(depth):
        i = l // 2
        if l % 2 == 0:
            p = _prep_ab(ab_w_in[i], mla_lat_gain[i], mla_w_uq[i], mla_w_ukv[i], mla_qk_gain[i], ab_w_out[i])
        else:
            p = _prep_cd(cd_w_in[i], fox_f_bias[i], cd_qk_gain[i], cd_w_out[i])
            p["rel_bias"] = _rel_bias(band_rel_bias[i])
        g = norm_gain[l]
        p["g_ffn1"], p["g_mix"], p["g_ffn2"] = g[0:1], g[1:2], g[2:3]
        layers.append(p)
    ffn_w = tuple(_to_bf16(w) for w in (ffn_w_gate, ffn_w_up, ffn_w_down))

    caches = (cache_mla_ckv, cache_mla_krope, cache_sb_k, cache_sb_v,
              cache_band_k, cache_band_v, cache_fox_k, cache_fox_v, cache_fox_logf)
    y_p, st_p = _trunk(x_prompt, 0, None, band_len, layers, ffn_w)
    y_s, st_s = _trunk(x_sample, past_len, caches, band_len, layers, ffn_w)
    out = [y_p, y_s]
    for a, c in zip(st_p, st_s):
        out += [a, c]
    return tuple(out)
```

```python
import functools

import jax
import jax.numpy as jnp
from jax import lax
from jax.experimental import pallas as pl
from jax.experimental.pallas import tpu as pltpu

F32 = jnp.float32
BF16 = jnp.bfloat16

EPS = 1e-6
NEG_INF = -1e30
CHUNK = 64
BAND_LEFT = 8
REL_CLIP = 128
N_HEADS = 8
HEAD_DIM = 128
Q_RANK = 512
KV_RANK = 256
ROPE_DIM = 64
NOPE_DIM = 128
QK_HEAD = NOPE_DIM + ROPE_DIM
MLA_QK_PAD = 256
ROPE_THETA = 10000.0
MLA_SCALE = QK_HEAD ** -0.5
ATT_SCALE = HEAD_DIM ** -0.5
LOG2E = 1.4426950408889634
HEAD_PAIR = 2
SB_HEADS = 4
STEP_SB_GROUP = 2
HW = N_HEADS * HEAD_DIM
AB_COLS = 4096
CD_COLS = 6400
BAND_KEYS = (BAND_LEFT + 1) * CHUNK

VMEM_LIMIT_BYTES = 48 * 1024 * 1024
FFN_VMEM_LIMIT_BYTES = 56 * 1024 * 1024


def _params(*sem):
    return pltpu.CompilerParams(dimension_semantics=sem, vmem_limit_bytes=VMEM_LIMIT_BYTES)


def _rms(x, gain, inv_n=None):
    if inv_n is None:
        ms = jnp.mean(x * x, axis=-1, keepdims=True)
    else:
        ms = jnp.sum(x * x, axis=-1, keepdims=True) * inv_n
    return x * lax.rsqrt(ms + EPS) * gain


def _dot(a, b):
    return jnp.dot(a, b, preferred_element_type=F32)


def _lane_tile(x, n):
    return x if n == 1 else jnp.concatenate([x] * n, axis=1)


def _dot_t(a, b):
    return lax.dot_general(a, b, (((1,), (1,)), ((), ())), preferred_element_type=F32)


def _ffn_kernel(x_ref, g_ref, wg_ref, wu_ref, wd_ref, o_ref, h_ref, *, n_f):
    j = pl.program_id(1)

    @pl.when(j == 0)
    def _():
        h_ref[...] = _rms(x_ref[...], g_ref[...]).astype(BF16)
        o_ref[...] = jnp.zeros_like(o_ref)

    h = h_ref[...]
    g = _dot(h, wg_ref[...])
    u = _dot(h, wu_ref[...])
    a = (g / (1.0 + jnp.exp(-g)) * u).astype(BF16)
    o_ref[...] += _dot(a, wd_ref[...])

    @pl.when(j == n_f - 1)
    def _():
        o_ref[...] = x_ref[...] + 0.5 * o_ref[...]


def _ffn(x, gain, wg, wu, wd, l, s):
    n, d = x.shape
    f = wg.shape[-1]
    tm = min(1024, n)
    tf = 512 if f % 512 == 0 else f
    n_f = f // tf
    return pl.pallas_call(
        functools.partial(_ffn_kernel, n_f=n_f),
        grid=(n // tm, n_f),
        in_specs=[
            pl.BlockSpec((tm, d), lambda i, j: (i, 0)),
            pl.BlockSpec((1, d), lambda i, j: (0, 0)),
            pl.BlockSpec((None, None, d, tf), lambda i, j: (l, s, 0, j)),
            pl.BlockSpec((None, None, d, tf), lambda i, j: (l, s, 0, j)),
            pl.BlockSpec((None, None, tf, d), lambda i, j: (l, s, j, 0)),
        ],
        out_specs=pl.BlockSpec((tm, d), lambda i, j: (i, 0)),
        out_shape=jax.ShapeDtypeStruct((n, d), F32),
        scratch_shapes=[pltpu.VMEM((tm, d), BF16)],
        compiler_params=pltpu.CompilerParams(dimension_semantics=("parallel", "arbitrary"),
                                             vmem_limit_bytes=FFN_VMEM_LIMIT_BYTES),
        name="ffn",
    )(x, gain, wg, wu, wd)


def _cast_kernel(x_ref, o_ref):
    o_ref[...] = x_ref[...].astype(BF16)


def _to_bf16(w):
    shape = w.shape
    w2 = w.reshape(-1, shape[-1])
    r, c = w2.shape
    tr = 256 if r % 256 == 0 else r
    out = pl.pallas_call(
        _cast_kernel,
        grid=(r // tr,),
        in_specs=[pl.BlockSpec((tr, c), lambda i: (i, 0))],
        out_specs=pl.BlockSpec((tr, c), lambda i: (i, 0)),
        out_shape=jax.ShapeDtypeStruct((r, c), BF16),
        compiler_params=_params("parallel"),
        name="to_bf16",
    )(w2)
    return out.reshape(shape)


def _proj_in_kernel(x_ref, g_ref, w_ref, o_ref, h_ref):
    @pl.when(pl.program_id(1) == 0)
    def _():
        h_ref[...] = _rms(x_ref[...], g_ref[...]).astype(BF16)

    o_ref[...] = _dot(h_ref[...], w_ref[...])


def _proj_in(x, gain, w, tn):
    n, d = x.shape
    c = w.shape[1]
    tm = min(1024, n)
    return pl.pallas_call(
        _proj_in_kernel,
        grid=(n // tm, c // tn),
        in_specs=[
            pl.BlockSpec((tm, d), lambda i, j: (i, 0)),
            pl.BlockSpec((1, d), lambda i, j: (0, 0)),
            pl.BlockSpec((d, tn), lambda i, j: (0, j)),
        ],
        out_specs=pl.BlockSpec((tm, tn), lambda i, j: (i, j)),
        out_shape=jax.ShapeDtypeStruct((n, c), F32),
        scratch_shapes=[pltpu.VMEM((tm, d), BF16)],
        compiler_params=_params("parallel", "arbitrary"),
        name="proj_in",
    )(x, gain, w)


def _proj_out_kernel(a_ref, b_ref, wa_ref, wb_ref, x_ref, o_ref):
    o_ref[...] = x_ref[...] + _dot(a_ref[...], wa_ref[...]) + _dot(b_ref[...], wb_ref[...])


def _proj_out(oa, ob, wa, wb, x):
    n, d = x.shape
    k = oa.shape[1]
    tm = min(1024, n)
    tn = min(1024, d)
    return pl.pallas_call(
        _proj_out_kernel,
        grid=(n // tm, d // tn),
        in_specs=[
            pl.BlockSpec((tm, k), lambda i, j: (i, 0)),
            pl.BlockSpec((tm, k), lambda i, j: (i, 0)),
            pl.BlockSpec((k, tn), lambda i, j: (0, j)),
            pl.BlockSpec((k, tn), lambda i, j: (0, j)),
            pl.BlockSpec((tm, tn), lambda i, j: (i, j)),
        ],
        out_specs=pl.BlockSpec((tm, tn), lambda i, j: (i, j)),
        out_shape=jax.ShapeDtypeStruct((n, d), F32),
        compiler_params=_params("parallel", "parallel"),
        name="proj_out",
    )(oa, ob, wa, wb, x)


def _rope_slab(x, cs):
    c = cs[:, 0:128]
    s1 = cs[:, 128:256]
    s2 = cs[:, 256:384]
    return x * c + pltpu.roll(x, 96, 1) * s1 + pltpu.roll(x, 32, 1) * s2


def _in_place(first_input, out_ids, targets):
    arrays, aliases = [], {}
    for t, o in zip(targets, out_ids):
        if not isinstance(t, jax.ShapeDtypeStruct):
            aliases[first_input + len(arrays)] = o
            arrays.append(t)
    return arrays, [pl.BlockSpec(memory_space=pl.ANY)] * len(arrays), aliases


def _store_heads(dst_ref, x):
    m = x.shape[0]
    for h in range(N_HEADS):
        dst_ref[pl.ds(h, m, stride=N_HEADS), :] = x[:, h * HEAD_DIM:(h + 1) * HEAD_DIM]


def _ab_post_kernel(sq_ref, sk_ref, sv_ref, tail_ref, cs_ref, gn_ref, wuq_ref, *rest):
    q_ref, ckv_ref, kr_ref, sqb_ref, skb_ref, svb_ref, skf_ref, svf_ref = rest[-8:]
    sqb_ref[...] = (sq_ref[...] * (ATT_SCALE * LOG2E)).astype(BF16)
    sk = sk_ref[...]
    sv = sv_ref[...]
    _store_heads(skf_ref, sk)
    _store_heads(svf_ref, sv)
    skb_ref[...] = sk.astype(BF16)
    svb_ref[...] = sv.astype(BF16)

    cs = cs_ref[...]
    g_cq = gn_ref[0:1, :]
    g_ckv = gn_ref[1:2, 0:KV_RANK]
    gq_n = gn_ref[2:3, 0:128]
    gq_r = gn_ref[3:4, 0:128]
    gk_r = gn_ref[4:5, 0:128]

    cq = _rms(tail_ref[:, 0:Q_RANK], g_cq)
    ckv_ref[...] = _rms(tail_ref[:, Q_RANK:Q_RANK + KV_RANK], g_ckv)
    kr = _rms(tail_ref[:, 768:896], gk_r, 1.0 / ROPE_DIM)
    kr_ref[...] = _rope_slab(kr, cs)[:, 0:ROPE_DIM]

    q = _dot(cq.astype(BF16), wuq_ref[...])
    for h in range(N_HEADS):
        lo = h * MLA_QK_PAD
        qn = _rms(q[:, lo:lo + 128], gq_n)
        qr = _rope_slab(_rms(q[:, lo + 128:lo + 256], gq_r, 1.0 / ROPE_DIM), cs)
        q_ref[:, lo:lo + 128] = (qn * (MLA_SCALE * LOG2E)).astype(BF16)
        q_ref[:, lo + 128:lo + 256] = (qr * (MLA_SCALE * LOG2E)).astype(BF16)


def _ab_post(proj, cs_tab, gains, wuq, layer, stacks):
    n = proj.shape[0]
    tm = min(256, n, cs_tab.shape[0])
    n_tab = cs_tab.shape[0] // tm
    row = lambda i: (i, 0)
    slot = lambda i: (layer, i, 0)
    b16 = lambda w: jax.ShapeDtypeStruct((n, w), BF16)
    same = lambda a: jax.ShapeDtypeStruct(a.shape, a.dtype)
    arrays, any_specs, aliases = _in_place(7, (1, 2, 6, 7), stacks)
    return pl.pallas_call(
        _ab_post_kernel,
        grid=(n // tm,),
        in_specs=[
            pl.BlockSpec((tm, HW), lambda i: (i, 0)),
            pl.BlockSpec((tm, HW), lambda i: (i, 1)),
            pl.BlockSpec((tm, HW), lambda i: (i, 2)),
            pl.BlockSpec((tm, HW), lambda i: (i, 3)),
            pl.BlockSpec((tm, 384), lambda i: (i % n_tab, 0)),
            pl.BlockSpec(gains.shape, lambda i: (0, 0)),
            pl.BlockSpec(wuq.shape, lambda i: (0, 0)),
        ] + any_specs,
        out_specs=[
            pl.BlockSpec((tm, N_HEADS * MLA_QK_PAD), row),
            pl.BlockSpec((None, tm, KV_RANK), slot),
            pl.BlockSpec((None, tm, ROPE_DIM), slot),
            pl.BlockSpec((tm, HW), row),
            pl.BlockSpec((tm, HW), row),
            pl.BlockSpec((tm, HW), row),
            pl.BlockSpec((None, tm * N_HEADS, HEAD_DIM), slot),
            pl.BlockSpec((None, tm * N_HEADS, HEAD_DIM), slot),
        ],
        out_shape=[b16(N_HEADS * MLA_QK_PAD), same(stacks[0]), same(stacks[1]), b16(HW), b16(HW), b16(HW),
                   same(stacks[2]), same(stacks[3])],
        input_output_aliases=aliases,
        compiler_params=_params("parallel"),
        name="ab_post",
    )(proj, proj, proj, proj, cs_tab, gains, wuq, *arrays)


def _mla_kv_kernel(ckv_ref, kr_ref, w_ref, g_ref, k_ref, v_ref):
    kv = _dot(ckv_ref[...].astype(BF16), w_ref[...])
    m = kv.shape[0]
    kr = jnp.concatenate([kr_ref[...], jnp.zeros((m, 128 - ROPE_DIM), F32)], axis=1).astype(BF16)
    g = g_ref[...]
    for h in range(N_HEADS):
        lo = h * 256
        k_ref[:, lo:lo + 128] = _rms(kv[:, lo:lo + 128], g).astype(BF16)
        k_ref[:, lo + 128:lo + 256] = kr
        v_ref[:, h * 128:(h + 1) * 128] = kv[:, lo + 128:lo + 256].astype(BF16)


def _mla_kv(ckv, kr, layer, w_ukv, gk_n):
    m = ckv.shape[1]
    tm = min(512, m)
    row = lambda i: (i, 0)
    return pl.pallas_call(
        _mla_kv_kernel,
        grid=(m // tm,),
        in_specs=[
            pl.BlockSpec((None, tm, KV_RANK), lambda i: (layer, i, 0)),
            pl.BlockSpec((None, tm, ROPE_DIM), lambda i: (layer, i, 0)),
            pl.BlockSpec(w_ukv.shape, lambda i: (0, 0)),
            pl.BlockSpec((1, 128), lambda i: (0, 0)),
        ],
        out_specs=[
            pl.BlockSpec((tm, N_HEADS * MLA_QK_PAD), row),
            pl.BlockSpec((tm, HW), row),
        ],
        out_shape=[
            jax.ShapeDtypeStruct((m, N_HEADS * MLA_QK_PAD), BF16),
            jax.ShapeDtypeStruct((m, HW), BF16),
        ],
        compiler_params=_params("parallel"),
        name="mla_kv",
    )(ckv, kr, w_ukv, gk_n)


def _cd_post_kernel(bq_ref, bk_ref, bv_ref, fq_ref, fk_ref, fv_ref, fl_ref, gn_ref, fb_ref, *rest, pad_tiles):
    bqb_ref, bkb_ref, bvb_ref, fqb_ref, fkb_ref, fvb_ref, bkt_ref, bvt_ref, fks_ref, fvs_ref, lf_ref = rest[-11:]
    live = pl.program_id(1) >= pad_tiles
    m = bq_ref.shape[0]
    bv = bv_ref[...]
    _store_heads(bvt_ref, bv)
    bvb_ref[...] = jnp.where(live, bv, 0.0).astype(BF16)
    fv = fv_ref[...]
    _store_heads(fvs_ref, fv)
    fvb_ref[...] = fv.astype(BF16)
    for h in range(N_HEADS):
        sl = slice(h * 128, (h + 1) * 128)
        rows = pl.ds(h, m, stride=N_HEADS)
        bqb_ref[:, sl] = (_rms(bq_ref[:, sl], gn_ref[0:1, :]) * (ATT_SCALE * LOG2E)).astype(BF16)
        bk = _rms(bk_ref[:, sl], gn_ref[1:2, :])
        bkt_ref[rows, :] = bk
        bkb_ref[:, sl] = jnp.where(live, bk, 0.0).astype(BF16)
        fqb_ref[:, sl] = (_rms(fq_ref[:, sl], gn_ref[2:3, :]) * (ATT_SCALE * LOG2E)).astype(BF16)
        fk = _rms(fk_ref[:, sl], gn_ref[3:4, :])
        fks_ref[rows, :] = fk
        fkb_ref[:, sl] = fk.astype(BF16)
    z = fl_ref[:, 0:128] + fb_ref[...]
    lf = -(jnp.maximum(-z, 0.0) + jnp.log1p(jnp.exp(-jnp.abs(z))))
    lf_ref[...] = lf[:, 0:N_HEADS]


def _cd_post(proj, gains, f_bias, b, t, pad_rows, layer, tail_slot, tail_rows, states):
    n = proj.shape[0]
    tm = min(256, t)
    tpb = t // tm
    pad_tiles = pad_rows // tm
    tail_tiles = tail_rows // tm
    tile = lambda s: jnp.maximum(s - pad_tiles, 0)
    src = lambda bi, s: bi * tpb + tile(s)
    tail = lambda bi, s: (tail_slot, bi * tail_tiles + jnp.maximum(tile(s) - (tpb - tail_tiles), 0), 0)
    col = lambda c: pl.BlockSpec((tm, HW), lambda bi, s, c=c: (src(bi, s), c))
    b16o = jax.ShapeDtypeStruct((n, HW), BF16)
    padded = jax.ShapeDtypeStruct((b * (pad_rows + t), HW), BF16)
    blk = pl.BlockSpec((tm, HW), lambda bi, s: (src(bi, s), 0))
    pblk = pl.BlockSpec((tm, HW), lambda bi, s: (bi * (tpb + pad_tiles) + s, 0))
    heads_blk = lambda imap: pl.BlockSpec((None, tm * N_HEADS, HEAD_DIM), imap)
    slot = lambda bi, s: (layer, src(bi, s), 0)
    same = lambda a: jax.ShapeDtypeStruct(a.shape, a.dtype)
    arrays, any_specs, aliases = _in_place(9, (6, 7, 8, 9, 10), states)
    return pl.pallas_call(
        functools.partial(_cd_post_kernel, pad_tiles=pad_tiles),
        grid=(b, tpb + pad_tiles),
        in_specs=[col(0), col(1), col(2), col(3), col(4), col(5),
                  pl.BlockSpec((tm, 256), lambda bi, s: (src(bi, s), 6 * HW // 256)),
                  pl.BlockSpec((4, 128), lambda bi, s: (0, 0)),
                  pl.BlockSpec((1, 128), lambda bi, s: (0, 0))] + any_specs,
        out_specs=[blk, pblk, pblk, blk, blk, blk,
                   heads_blk(tail), heads_blk(tail), heads_blk(slot), heads_blk(slot),
                   pl.BlockSpec((None, tm, N_HEADS), slot)],
        out_shape=[b16o, padded, padded, b16o, b16o, b16o] + [same(a) for a in states],
        input_output_aliases=aliases,
        compiler_params=_params("arbitrary", "arbitrary"),
        name="cd_post",
    )(proj, proj, proj, proj, proj, proj, proj, gains, f_bias, *arrays)


def _band_buf_kernel(cache_ref, new_ref, *rest):
    o_ref = rest[-1]
    keep = cache_ref.shape[0] - new_ref.shape[0]
    o_ref[0:keep, :] = cache_ref[new_ref.shape[0]:, :]
    o_ref[keep:, :] = new_ref[...]


def _band_buf(cache, new, layer, out):
    _, b, rows, _ = cache.shape
    blk = pl.BlockSpec((None, None, rows, HEAD_DIM), lambda i: (layer, i, 0, 0))
    arrays, any_specs, aliases = _in_place(2, (0,), (out,))
    return pl.pallas_call(
        _band_buf_kernel,
        grid=(b,),
        in_specs=[blk, pl.BlockSpec((None, new.shape[1], HEAD_DIM), lambda i: (i, 0, 0))] + any_specs,
        out_specs=blk,
        out_shape=jax.ShapeDtypeStruct(out.shape, out.dtype),
        input_output_aliases=aliases,
        compiler_params=_params("parallel"),
        name="band_buf",
    )(cache, new, *arrays)


def _cumsum_kernel(x_ref, o_ref, *, n_blk, scale):
    jj = lax.broadcasted_iota(jnp.int32, (128, 128), 0)
    ss = lax.broadcasted_iota(jnp.int32, (128, 128), 1)
    upper = jnp.where(jj <= ss, 1.0, 0.0).astype(BF16)

    def body(i, carry):
        x = x_ref[i]
        hi = x.astype(BF16)
        r1 = x - hi.astype(F32)
        mid = r1.astype(BF16)
        lo = (r1 - mid.astype(F32)).astype(BF16)
        c = _dot(hi, upper) + _dot(mid, upper) + _dot(lo, upper) + carry
        o_ref[i] = c * scale
        return c[:, 127:128]

    lax.fori_loop(0, n_blk, body, jnp.zeros((N_HEADS, 1), F32))


def _cumsum_rows(lf, scale):
    b, length, nh = lf.shape
    n_blk = -(-length // 128)
    x = jnp.pad(jnp.swapaxes(lf, 1, 2), ((0, 0), (0, 0), (0, n_blk * 128 - length)))
    x = jnp.swapaxes(x.reshape(b, nh, n_blk, 128), 1, 2)
    out = pl.pallas_call(
        functools.partial(_cumsum_kernel, n_blk=n_blk, scale=scale),
        grid=(b,),
        in_specs=[pl.BlockSpec((None, n_blk, nh, 128), lambda i: (i, 0, 0, 0))],
        out_specs=pl.BlockSpec((None, n_blk, nh, 128), lambda i: (i, 0, 0, 0)),
        out_shape=jax.ShapeDtypeStruct((b, n_blk, nh, 128), F32),
        compiler_params=_params("parallel"),
        name="cumsum",
    )(x)
    return jnp.swapaxes(out, 1, 2).reshape(b, nh, n_blk * 128)[:, :, :length]


def _rel_bias_kernel(tab_ref, o_ref, *, n_tab):
    h = pl.program_id(0)
    i = lax.broadcasted_iota(jnp.int32, (CHUNK, BAND_KEYS), 0)
    j = lax.broadcasted_iota(jnp.int32, (CHUNK, BAND_KEYS), 1)
    idx = jnp.clip(BAND_LEFT * CHUNK + i - j, -REL_CLIP, REL_CLIP) + REL_CLIP

    def body(r, acc):
        return jnp.where(idx == r, tab_ref[h, r] * LOG2E, acc)

    o_ref[...] = lax.fori_loop(0, n_tab, body, jnp.zeros((CHUNK, BAND_KEYS), F32))


def _rel_bias(table):
    nh, n_tab = table.shape
    return pl.pallas_call(
        functools.partial(_rel_bias_kernel, n_tab=n_tab),
        grid=(nh,),
        in_specs=[pl.BlockSpec(memory_space=pltpu.SMEM)],
        out_specs=pl.BlockSpec((None, CHUNK, BAND_KEYS), lambda h: (h, 0, 0)),
        out_shape=jax.ShapeDtypeStruct((nh, CHUNK, BAND_KEYS), F32),
        compiler_params=_params("arbitrary"),
        name="rel_bias",
    )(table)


def _flash_kernel(*refs, tq, mode, has_bias):
    if has_bias:
        q_ref, k_ref, v_ref, cq_ref, ck_ref, o_ref, m_scr, acc_scr = refs
    else:
        q_ref, k_ref, v_ref, o_ref, m_scr, acc_scr = refs
    hg = pl.program_id(1)
    qi = pl.program_id(2)
    dk = q_ref.shape[1] // HEAD_PAIR
    reps = tq // HEAD_DIM
    m_scr[...] = jnp.full(m_scr.shape, NEG_INF, F32)
    acc_scr[...] = jnp.zeros_like(acc_scr)
    ones = jnp.ones((tq, HEAD_DIM), BF16)
    if has_bias:
        lane = lax.broadcasted_iota(jnp.int32, cq_ref.shape, 1)
        cqs = [jnp.broadcast_to(
            jnp.sum(jnp.where(lane == hg * HEAD_PAIR + j, cq_ref[...], 0.0), axis=-1, keepdims=True),
            (tq, HEAD_DIM)) for j in range(HEAD_PAIR)]

    def step(kbs, diag_last):
        chains = [(kb, j) for kb in kbs for j in range(HEAD_PAIR)]
        scores = [_dot_t(q_ref[:, j * dk:(j + 1) * dk],
                         k_ref[pl.ds(pl.multiple_of(kb * tq, tq), tq), j * dk:(j + 1) * dk]) for kb, j in chains]
        for (kb, j), s in zip(chains, scores):
            ks = pl.multiple_of(kb * tq, tq)
            masked = diag_last and kb is kbs[-1]
            if has_bias:
                s = s - ck_ref[j, pl.ds(kb, 1), :]
            if masked:
                row = lax.broadcasted_iota(jnp.int32, (tq, tq), 0)
                col = lax.broadcasted_iota(jnp.int32, (tq, tq), 1)
                if mode == "chunk":
                    ok = lax.shift_right_logical(col, 6) <= lax.shift_right_logical(row, 6)
                else:
                    ok = col <= row
                s = jnp.where(ok, s, NEG_INF)
            m_old = m_scr[j]
            rmax = jnp.max(s, axis=-1, keepdims=True)
            if has_bias:
                m_new = jnp.maximum(m_old, rmax + cqs[j])
                shift = m_new - cqs[j]
            else:
                m_new = jnp.maximum(m_old, rmax)
                shift = m_new
            p = jnp.exp2(s - _lane_tile(shift, reps))
            alpha = jnp.exp2(m_old - m_new)
            v1 = jnp.concatenate([v_ref[pl.ds(ks, tq), j * HEAD_DIM:(j + 1) * HEAD_DIM], ones], axis=1)
            acc_scr[j] = _lane_tile(alpha, 2) * acc_scr[j] + _dot(p.astype(BF16), v1)
            m_scr[j] = m_new

    def body(i, c):
        step([2 * i, 2 * i + 1], False)
        return c

    lax.fori_loop(0, lax.shift_right_logical(qi, 1), body, 0)

    @pl.when(lax.bitwise_and(qi, 1) == 1)
    def _():
        step([qi - 1, qi], True)

    @pl.when(lax.bitwise_and(qi, 1) == 0)
    def _():
        step([qi], True)

    for j in range(HEAD_PAIR):
        acc = acc_scr[j]
        o_ref[:, j * HEAD_DIM:(j + 1) * HEAD_DIM] = (acc[:, :HEAD_DIM] / acc[:, HEAD_DIM:]).astype(BF16)


def _flash(q, k, v, mode, cq=None, ck=None):
    b, t, _ = q.shape
    dk = q.shape[2] // N_HEADS
    tq = min(512, t)
    nq = t // tq
    has_bias = cq is not None
    hp = HEAD_PAIR
    in_specs = [
        pl.BlockSpec((None, tq, hp * dk), lambda bi, h, qi: (bi, qi, h)),
        pl.BlockSpec((None, t, hp * dk), lambda bi, h, qi: (bi, 0, h)),
        pl.BlockSpec((None, t, hp * HEAD_DIM), lambda bi, h, qi: (bi, 0, h)),
    ]
    args = [q, k, v]
    if has_bias:
        in_specs += [
            pl.BlockSpec((None, tq, N_HEADS), lambda bi, h, qi: (bi, qi, 0)),
            pl.BlockSpec((None, hp, nq, tq), lambda bi, h, qi: (bi, h, 0, 0)),
        ]
        args += [cq, ck.reshape(b, N_HEADS, nq, tq)]
    return pl.pallas_call(
        functools.partial(_flash_kernel, tq=tq, mode=mode, has_bias=has_bias),
        grid=(b, N_HEADS // hp, nq),
        in_specs=in_specs,
        out_specs=pl.BlockSpec((None, tq, hp * HEAD_DIM), lambda bi, h, qi: (bi, qi, h)),
        out_shape=jax.ShapeDtypeStruct((b, t, HW), BF16),
        scratch_shapes=[pltpu.VMEM((hp, tq, HEAD_DIM), F32), pltpu.VMEM((hp, tq, 2 * HEAD_DIM), F32)],
        compiler_params=_params("parallel", "parallel", "arbitrary"),
        name="flash_" + mode,
    )(*args)


def _sb_stage1(z, tri, ok):
    sp = jnp.maximum(z, 0.0) + jnp.log2(1.0 + jnp.exp2(-jnp.abs(z)))
    lsig = z - sp
    if ok is not None:
        sp = jnp.where(ok, sp, 0.0)
    right = _dot(sp.astype(BF16), tri)
    return lsig, right, jnp.sum(sp, axis=-1, keepdims=True)


def _sb_stage2(lsig, right, c, v, ok):
    n = lsig.shape[1]
    c_wide = _lane_tile(c, n // HEAD_DIM) if n % HEAD_DIM == 0 else c[:, :n]
    w = jnp.exp2(lsig - right + c_wide)
    if ok is not None:
        w = jnp.where(ok, w, 0.0)
    return _dot(w.astype(BF16), v)


def _tri(n):
    jj = lax.broadcasted_iota(jnp.int32, (n, n), 0)
    ss = lax.broadcasted_iota(jnp.int32, (n, n), 1)
    return jnp.where(jj > ss, 1.0, 0.0).astype(BF16)


def _sb_kernel(q_ref, k_ref, v_ref, o_ref, c_scr, acc_scr, *, tq, sub):
    qi = pl.program_id(2)
    n_sub = tq // sub
    tri = _tri(sub)
    c_scr[...] = jnp.zeros_like(c_scr)
    acc_scr[...] = jnp.zeros_like(acc_scr)

    def block(kb, diag):
        chains = [(d, j) for d in reversed(range(n_sub)) for j in range(SB_HEADS)]
        hs = lambda j: slice(j * HEAD_DIM, (j + 1) * HEAD_DIM)
        ks = lambda d: pl.multiple_of(kb * tq + d * sub, sub)
        oks = {}
        for d in range(n_sub):
            oks[d] = None
            if diag:
                row = lax.broadcasted_iota(jnp.int32, (tq, sub), 0)
                col = lax.broadcasted_iota(jnp.int32, (tq, sub), 1)
                oks[d] = col + d * sub < row
        zs = {(d, j): _dot_t(q_ref[:, hs(j)], k_ref[pl.ds(ks(d), sub), hs(j)]) for d, j in chains}
        mids = {dj: _sb_stage1(zs[dj], tri, oks[dj[0]]) for dj in chains}
        for d, j in chains:
            lsig, right, rowsum = mids[(d, j)]
            c = c_scr[j]
            acc_scr[j] += _sb_stage2(lsig, right, c, v_ref[pl.ds(ks(d), sub), hs(j)], oks[d])
            c_scr[j] = c - rowsum

    block(qi, True)

    def body(i, c):
        block(qi - 1 - i, False)
        return c

    lax.fori_loop(0, qi, body, 0)
    for j in range(SB_HEADS):
        o_ref[:, j * HEAD_DIM:(j + 1) * HEAD_DIM] = acc_scr[j].astype(BF16)


def _sb_prompt(q, k, v):
    b, t, _ = q.shape
    tq = min(512, t)
    sub = min(256, tq)
    hp = SB_HEADS
    return pl.pallas_call(
        functools.partial(_sb_kernel, tq=tq, sub=sub),
        grid=(b, N_HEADS // hp, t // tq),
        in_specs=[
            pl.BlockSpec((None, tq, hp * HEAD_DIM), lambda bi, h, qi: (bi, qi, h)),
            pl.BlockSpec((None, t, hp * HEAD_DIM), lambda bi, h, qi: (bi, 0, h)),
            pl.BlockSpec((None, t, hp * HEAD_DIM), lambda bi, h, qi: (bi, 0, h)),
        ],
        out_specs=pl.BlockSpec((None, tq, hp * HEAD_DIM), lambda bi, h, qi: (bi, qi, h)),
        out_shape=jax.ShapeDtypeStruct((b, t, HW), BF16),
        scratch_shapes=[pltpu.VMEM((hp, tq, HEAD_DIM), F32), pltpu.VMEM((hp, tq, HEAD_DIM), F32)],
        compiler_params=_params("parallel", "parallel", "arbitrary"),
        name="sb_prompt",
    )(q, k, v)


BAND_PAIR_KEYS = BAND_KEYS + CHUNK


def _band_kernel(q_ref, k_ref, v_ref, b_ref, o_ref, *, n_pairs, unroll):
    rows = 2 * CHUNK
    col = lax.broadcasted_iota(jnp.int32, (rows, BAND_PAIR_KEYS), 1)
    ones = jnp.ones((BAND_PAIR_KEYS, HEAD_DIM), BF16)

    def body(i, carry):
        chains = [(u, j) for u in range(unroll) for j in range(HEAD_PAIR)]
        hs = lambda j: slice(j * HEAD_DIM, (j + 1) * HEAD_DIM)
        qs = lambda u: pl.multiple_of((i * unroll + u) * rows, rows)
        scores = {(u, j): _dot_t(q_ref[pl.ds(qs(u), rows), hs(j)], k_ref[pl.ds(qs(u), BAND_PAIR_KEYS), hs(j)])
                  for u, j in chains}
        for u, j in chains:
            ok = col >= (BAND_LEFT - 2 * (i * unroll + u)) * CHUNK
            s = jnp.where(ok, scores[(u, j)] + b_ref[j], NEG_INF)
            p = jnp.exp2(s - jnp.max(s, axis=-1, keepdims=True))
            r = _dot(p.astype(BF16), jnp.concatenate([v_ref[pl.ds(qs(u), BAND_PAIR_KEYS), hs(j)], ones], axis=1))
            o_ref[pl.ds(qs(u), rows), hs(j)] = (r[:, :HEAD_DIM] / r[:, HEAD_DIM:]).astype(BF16)
        return carry

    lax.fori_loop(0, n_pairs // unroll, body, 0)


def _band_prompt(q, k, v, bias):
    b, t, _ = q.shape
    tp = k.shape[1]
    hp = HEAD_PAIR
    n_pairs = t // (2 * CHUNK)
    assert t % (2 * CHUNK) == 0
    unroll = 2 if n_pairs % 2 == 0 else 1
    nh = bias.shape[0]
    blocked = jnp.full((nh, CHUNK, CHUNK), NEG_INF, F32)
    pair_bias = jnp.concatenate([jnp.concatenate([bias, blocked], axis=2),
                                 jnp.concatenate([blocked, bias], axis=2)], axis=1)
    return pl.pallas_call(
        functools.partial(_band_kernel, n_pairs=n_pairs, unroll=unroll),
        grid=(b, N_HEADS // hp),
        in_specs=[
            pl.BlockSpec((None, t, hp * HEAD_DIM), lambda bi, h: (bi, 0, h)),
            pl.BlockSpec((None, tp, hp * HEAD_DIM), lambda bi, h: (bi, 0, h)),
            pl.BlockSpec((None, tp, hp * HEAD_DIM), lambda bi, h: (bi, 0, h)),
            pl.BlockSpec((hp, 2 * CHUNK, BAND_PAIR_KEYS), lambda bi, h: (h, 0, 0)),
        ],
        out_specs=pl.BlockSpec((None, t, hp * HEAD_DIM), lambda bi, h: (bi, 0, h)),
        out_shape=jax.ShapeDtypeStruct((b, t, HW), BF16),
        compiler_params=_params("parallel", "parallel"),
        name="band_prompt",
    )(q, k, v, pair_bias)


def _step_softmax_kernel(*refs, mode):
    if mode == "fox":
        q_ref, kp_ref, vp_ref, kn_ref, vn_ref, cq_ref, ckp_ref, ckn_ref, o_ref = refs
    elif mode == "band":
        q_ref, kp_ref, vp_ref, kn_ref, vn_ref, bp_ref, bn_ref, o_ref = refs
    else:
        q_ref, kp_ref, vp_ref, kn_ref, vn_ref, w_ref, g_ref, o_ref = refs
    t = q_ref.shape[0]
    dk = q_ref.shape[1] // N_HEADS
    if mode == "mla":
        kv = _dot(kp_ref[...].astype(BF16), w_ref[...])
        kr = jnp.concatenate([vp_ref[...], jnp.zeros((kv.shape[0], 128 - ROPE_DIM), F32)], axis=1).astype(BF16)
        kp_h = lambda h: jnp.concatenate(
            [_rms(kv[:, h * 256:h * 256 + 128], g_ref[...]).astype(BF16), kr], axis=1)
        vp_h = lambda h: kv[:, h * 256 + 128:(h + 1) * 256].astype(BF16)
    else:
        p_rows = kp_ref.shape[0] // N_HEADS
        kp_h = lambda h: kp_ref[pl.ds(h, p_rows, stride=N_HEADS), :].astype(BF16)
        vp_h = lambda h: vp_ref[pl.ds(h, p_rows, stride=N_HEADS), :].astype(BF16)
    scores = []
    for h in range(N_HEADS):
        ks = slice(h * dk, (h + 1) * dk)
        scores.append((_dot_t(q_ref[:, ks], kp_h(h)), _dot_t(q_ref[:, ks], kn_ref[:, ks])))
    for h in range(N_HEADS):
        vs = slice(h * HEAD_DIM, (h + 1) * HEAD_DIM)
        sp, sn = scores[h]
        if mode == "fox":
            cq = cq_ref[:, h:h + 1]
            sp = sp + (cq - ckp_ref[h:h + 1, :])
            sn = sn + (cq - ckn_ref[h:h + 1, :])
            row = lax.broadcasted_iota(jnp.int32, (t, t), 0)
            col = lax.broadcasted_iota(jnp.int32, (t, t), 1)
            sn = jnp.where(col <= row, sn, NEG_INF)
        elif mode == "band":
            sp = sp + bp_ref[h]
            sn = sn + bn_ref[h]
        m = jnp.maximum(jnp.max(sp, axis=-1, keepdims=True), jnp.max(sn, axis=-1, keepdims=True))
        pp = jnp.exp2(sp - m)
        pn = jnp.exp2(sn - m)
        l = jnp.sum(pp, axis=-1, keepdims=True) + jnp.sum(pn, axis=-1, keepdims=True)
        o = _dot(pp.astype(BF16), vp_h(h)) + _dot(pn.astype(BF16), vn_ref[:, vs])
        o_ref[:, vs] = (o / l).astype(BF16)


def _cache_spec(cache, layer):
    return pl.BlockSpec((None, None) + cache.shape[2:], lambda i: (layer, i, 0, 0))


def _step_softmax(mode, layer, q, kp, vp, kn, vn, *extra):
    b, t, qw = q.shape
    full = lambda a: pl.BlockSpec((None,) + a.shape[1:], lambda i: (i,) + (0,) * (a.ndim - 1))
    cached = lambda a: _cache_spec(a, layer)
    in_specs = [full(q), cached(kp), cached(vp), full(kn), full(vn)]
    if mode == "fox":
        in_specs += [full(e) for e in extra]
    elif mode == "band":
        in_specs += [pl.BlockSpec(e.shape, lambda i: (0, 0, 0)) for e in extra]
    else:
        in_specs += [pl.BlockSpec(e.shape, lambda i: (0, 0)) for e in extra]
    return pl.pallas_call(
        functools.partial(_step_softmax_kernel, mode=mode),
        grid=(b,),
        in_specs=in_specs,
        out_specs=pl.BlockSpec((None, t, HW), lambda i: (i, 0, 0)),
        out_shape=jax.ShapeDtypeStruct((b, t, HW), BF16),
        compiler_params=_params("parallel"),
        name="step_" + mode,
    )(q, kp, vp, kn, vn, *extra)


def _step_sb_kernel(q_ref, kp_ref, vp_ref, kn_ref, vn_ref, o_ref, *, blk):
    t = q_ref.shape[0]
    p = kp_ref.shape[0] // N_HEADS
    tri_n = _tri(t)
    tri_p = _tri(blk)
    row = lax.broadcasted_iota(jnp.int32, (t, t), 0)
    col = lax.broadcasted_iota(jnp.int32, (t, t), 1)
    blocks = list(reversed(range(p // blk)))
    hsl = lambda h: slice(h * HEAD_DIM, (h + 1) * HEAD_DIM)
    rows = lambda kb, h: pl.ds(kb * blk * N_HEADS + h, blk, stride=N_HEADS)
    for h0 in range(0, N_HEADS, STEP_SB_GROUP):
        heads = range(h0, h0 + STEP_SB_GROUP)
        mids = {}
        for h in heads:
            mids[(h, None)] = _sb_stage1(_dot_t(q_ref[:, hsl(h)], kn_ref[:, hsl(h)]), tri_n, col < row)
            for kb in blocks:
                mids[(h, kb)] = _sb_stage1(_dot_t(q_ref[:, hsl(h)], kp_ref[rows(kb, h), :].astype(BF16)), tri_p, None)
        for h in heads:
            lsig, right, rowsum = mids[(h, None)]
            c = jnp.zeros((t, HEAD_DIM), F32)
            acc = _sb_stage2(lsig, right, c, vn_ref[:, hsl(h)], col < row)
            c = c - rowsum
            for kb in blocks:
                lsig, right, rowsum = mids[(h, kb)]
                acc = acc + _sb_stage2(lsig, right, c, vp_ref[rows(kb, h), :].astype(BF16), None)
                c = c - rowsum
            o_ref[:, hsl(h)] = acc.astype(BF16)


def _step_sb(layer, q, kp, vp, kn, vn):
    b, t, _ = q.shape
    p = kp.shape[2] // N_HEADS
    blk = min(256, p)
    full = lambda a: pl.BlockSpec((None,) + a.shape[1:], lambda i: (i, 0, 0))
    return pl.pallas_call(
        functools.partial(_step_sb_kernel, blk=blk),
        grid=(b,),
        in_specs=[full(q), _cache_spec(kp, layer), _cache_spec(vp, layer), full(kn), full(vn)],
        out_specs=pl.BlockSpec((None, t, HW), lambda i: (i, 0, 0)),
        out_shape=jax.ShapeDtypeStruct((b, t, HW), BF16),
        compiler_params=_params("parallel"),
        name="step_sb",
    )(q, kp, vp, kn, vn)


def _rope_tables(pos, rows):
    half = ROPE_DIM // 2
    inv = ROPE_THETA ** (-jnp.arange(half, dtype=F32) / half)
    ang = pos.astype(F32)[:, None] * inv[None, :]
    c, s, z = jnp.cos(ang), jnp.sin(ang), jnp.zeros_like(ang)
    tab = jnp.concatenate([c, c, z, z, -s, z, z, z, z, s, z, z], axis=1)
    return jnp.tile(tab, (max(1, rows // tab.shape[0]), 1))


def _prep_ab(w_in, lat_gain, w_uq, w_ukv, qk_gain, w_out):
    d = w_in.shape[0]
    cq, ckv, kr, sq, sk, sv = jnp.split(w_in, [512, 768, 832, 1856, 2880], axis=1)
    w_in_p = jnp.concatenate([sq, sk, sv, cq, ckv, kr, jnp.zeros((d, AB_COLS - w_in.shape[1]), F32)], axis=1)
    uq = w_uq.reshape(Q_RANK, N_HEADS, QK_HEAD)
    uq = jnp.pad(uq, ((0, 0), (0, 0), (0, MLA_QK_PAD - QK_HEAD))).reshape(Q_RANK, N_HEADS * MLA_QK_PAD)
    gains = jnp.zeros((8, Q_RANK), F32)
    gains = gains.at[0, :].set(lat_gain[:Q_RANK])
    gains = gains.at[1, :KV_RANK].set(lat_gain[Q_RANK:])
    gains = gains.at[2, :NOPE_DIM].set(qk_gain[0, :NOPE_DIM])
    gains = gains.at[3, :ROPE_DIM].set(qk_gain[0, NOPE_DIM:])
    gains = gains.at[4, :ROPE_DIM].set(qk_gain[1, NOPE_DIM:])
    return dict(w_in=w_in_p.astype(BF16), w_uq=uq.astype(BF16), w_ukv=w_ukv.astype(BF16), gains=gains,
                gk_n=qk_gain[1:2, :NOPE_DIM], w_out_a=w_out[:HW].astype(BF16), w_out_b=w_out[HW:].astype(BF16))


def _prep_cd(w_in, f_bias, qk_gain, w_out):
    d = w_in.shape[0]
    w_in_p = jnp.pad(w_in, ((0, 0), (0, CD_COLS - w_in.shape[1])))
    fb = jnp.pad(f_bias, (0, 128 - N_HEADS)).reshape(1, 128)
    return dict(w_in=w_in_p.astype(BF16), gains=qk_gain, f_bias=fb,
                w_out_a=w_out[:HW].astype(BF16), w_out_b=w_out[HW:].astype(BF16))


def _heads_view(c):
    l, b, p, h, d = c.shape
    return c.reshape(l, b, p * h, d)


def _ab_mixer(x, b, t, pos, i, caches, states, p):
    n = x.shape[0]
    proj = _proj_in(x, p["g_mix"], p["w_in"], 1024)
    cs = _rope_tables(pos, min(256, n))
    q, ckv, kr, sq, skb, svb, sk, sv = _ab_post(proj, cs, p["gains"], p["w_uq"], i, states)
    kf, vf = _mla_kv(ckv, kr, i, p["w_ukv"], p["gk_n"])
    r3 = lambda a: a.reshape(b, t, a.shape[-1])
    if caches is None:
        o_mla = _flash(r3(q), r3(kf), r3(vf), "chunk")
        o_sb = _sb_prompt(r3(sq), r3(skb), r3(svb))
    else:
        c_ckv, c_kr, c_sk, c_sv = caches
        o_mla = _step_softmax("mla", i, r3(q), c_ckv, c_kr, r3(kf), r3(vf), p["w_ukv"], p["gk_n"])
        o_sb = _step_sb(i, r3(sq), _heads_view(c_sk), _heads_view(c_sv), r3(skb), r3(svb))
    x = _proj_out(o_mla.reshape(n, HW), o_sb.reshape(n, HW), p["w_out_a"], p["w_out_b"], x)
    return x, (ckv, kr, sk, sv)


def _cd_mixer(x, b, t, i, caches, band_len, states, p):
    n = x.shape[0]
    proj = _proj_in(x, p["g_mix"], p["w_in"], 1280)
    r3 = lambda a: a.reshape(b, -1, a.shape[-1])
    if caches is None:
        assert t >= band_len
        outs = _cd_post(proj, p["gains"], p["f_bias"], b, t, BAND_LEFT * CHUNK, i, i, band_len, states)
        bq, bkb, bvb, fq, fkb, fvb, bk_buf, bv_buf, fk, fv, lf = outs
        o_band = _band_prompt(r3(bq), r3(bkb), r3(bvb), p["rel_bias"])
        cum = _cumsum_rows(lf[i].reshape(b, t, N_HEADS), LOG2E)
        o_fox = _flash(r3(fq), r3(fkb), r3(fvb), "causal", jnp.swapaxes(cum, 1, 2), cum)
    else:
        c_bk, c_bv, c_fk, c_fv, c_lf = caches
        pl_ = c_fk.shape[2]
        bl = c_bk.shape[2]
        new_rows = jax.ShapeDtypeStruct((1, n * N_HEADS, HEAD_DIM), F32)
        outs = _cd_post(proj, p["gains"], p["f_bias"], b, t, 0, i, 0, t, (new_rows, new_rows) + states[2:])
        bq, bkb, bvb, fq, fkb, fvb, bk_new, bv_new, fk, fv, lf = outs
        o_band = _step_softmax("band", i, r3(bq), _heads_view(c_bk), _heads_view(c_bv), r3(bkb), r3(bvb),
                               p["rel_bias"][:, :, :bl], p["rel_bias"][:, :, bl:])
        shape4 = (states[0].shape[0], b, band_len * N_HEADS, HEAD_DIM)
        buf4 = lambda a: (jax.ShapeDtypeStruct(shape4, F32) if isinstance(a, jax.ShapeDtypeStruct)
                          else a.reshape(shape4))
        bk_buf = _band_buf(_heads_view(c_bk), bk_new.reshape(b, t * N_HEADS, HEAD_DIM), i, buf4(states[0]))
        bv_buf = _band_buf(_heads_view(c_bv), bv_new.reshape(b, t * N_HEADS, HEAD_DIM), i, buf4(states[1]))
        bk_buf, bv_buf = bk_buf.reshape(states[0].shape), bv_buf.reshape(states[1].shape)
        cum = _cumsum_rows(jnp.concatenate([c_lf[i].astype(F32), lf[i].reshape(b, t, N_HEADS)], 1), LOG2E)
        o_fox = _step_softmax("fox", i, r3(fq), _heads_view(c_fk), _heads_view(c_fv), r3(fkb), r3(fvb),
                              jnp.swapaxes(cum[:, :, pl_:], 1, 2), cum[:, :, :pl_], cum[:, :, pl_:])
    x = _proj_out(o_band.reshape(n, HW), o_fox.reshape(n, HW), p["w_out_a"], p["w_out_b"], x)
    return x, (bk_buf, bv_buf, fk, fv, lf)


def _trunk(x3, past_len, caches, band_len, layers, ffn_w):
    b, t, d = x3.shape
    n = b * t
    x = x3.reshape(n, d)
    pos = jnp.arange(past_len, past_len + t)
    n_ab, n_cd = (len(layers) + 1) // 2, len(layers) // 2
    stack = lambda *shape: jnp.zeros(shape, F32)
    ab = (stack(n_ab, n, KV_RANK), stack(n_ab, n, ROPE_DIM),
          stack(n_ab, n * N_HEADS, HEAD_DIM), stack(n_ab, n * N_HEADS, HEAD_DIM))
    cd = (stack(n_cd, b * band_len * N_HEADS, HEAD_DIM), stack(n_cd, b * band_len * N_HEADS, HEAD_DIM),
          stack(n_cd, n * N_HEADS, HEAD_DIM), stack(n_cd, n * N_HEADS, HEAD_DIM), stack(n_cd, n, N_HEADS))
    for l, p in enumerate(layers):
        i = l // 2
        x = _ffn(x, p["g_ffn1"], *ffn_w, l, 0)
        if l % 2 == 0:
            x, ab = _ab_mixer(x, b, t, pos, i, None if caches is None else caches[:4], ab, p)
        else:
            x, cd = _cd_mixer(x, b, t, i, None if caches is None else caches[4:], band_len, cd, p)
        x = _ffn(x, p["g_ffn2"], *ffn_w, l, 1)
    h5 = lambda a, rows: a.reshape(a.shape[0], b, rows, N_HEADS, HEAD_DIM)
    states = [ab[0].reshape(n_ab, b, t, KV_RANK), ab[1].reshape(n_ab, b, t, ROPE_DIM), h5(ab[2], t), h5(ab[3], t),
              h5(cd[0], band_len), h5(cd[1], band_len), h5(cd[2], t), h5(cd[3], t), cd[4].reshape(n_cd, b, t, N_HEADS)]
    return x.reshape(b, t, d), states


def kernel(x_prompt, x_sample, cache_mla_ckv, cache_mla_krope, cache_sb_k, cache_sb_v, cache_band_k,
           cache_band_v, cache_fox_k, cache_fox_v, cache_fox_logf, norm_gain, ffn_w_gate, ffn_w_up,
           ffn_w_down, ab_w_in, mla_lat_gain, mla_w_uq, mla_w_ukv, mla_qk_gain, ab_w_out, cd_w_in,
           fox_f_bias, cd_qk_gain, band_rel_bias, cd_w_out):
    depth = norm_gain.shape[0]
    past_len = cache_fox_k.shape[2]
    band_len = cache_band_k.shape[2]
    t_step = x_sample.shape[1]
    assert t_step == CHUNK and past_len % CHUNK == 0 and band_len == BAND_LEFT * CHUNK
    assert x_prompt.shape[1] % CHUNK == 0

    layers = []
    for l in range(depth):
        i = l // 2
        if l % 2 == 0:
            p = _prep_ab(ab_w_in[i], mla_lat_gain[i], mla_w_uq[i], mla_w_ukv[i], mla_qk_gain[i], ab_w_out[i])
        else:
            p = _prep_cd(cd_w_in[i], fox_f_bias[i], cd_qk_gain[i], cd_w_out[i])
            p["rel_bias"] = _rel_bias(band_rel_bias[i])
        g = norm_gain[l]
        p["g_ffn1"], p["g_mix"], p["g_ffn2"] = g[0:1], g[1:2], g[2:3]
        layers.append(p)
    ffn_w = tuple(_to_bf16(w) for w in (ffn_w_gate, ffn_w_up, ffn_w_down))

    caches = (cache_mla_ckv, cache_mla_krope, cache_sb_k, cache_sb_v,
              cache_band_k, cache_band_v, cache_fox_k, cache_fox_v, cache_fox_logf)
    y_p, st_p = _trunk(x_prompt, 0, None, band_len, layers, ffn_w)
    y_s, st_s = _trunk(x_sample, past_len, caches, band_len, layers, ffn_w)
    out = [y_p, y_s]
    for a, c in zip(st_p, st_s):
        out += [a, c]
    return tuple(out)
```
